```python
import math
import jax, jax.numpy as jnp
from jax import lax
import numpy as np

D_MODEL = 1024
BATCH = 8
SEQ = 8192
DEPTH = 1
DEC_BATCH = 2
DEC_SEQ = 16384
PAST_LEN = 128

MIX_WIDTH = D_MODEL
RET_WIDTH = MIX_WIDTH // 2
POOL_WIDTH = MIX_WIDTH - RET_WIDTH
N_RET_HEADS = 4
RET_HEAD_DIM = RET_WIDTH // N_RET_HEADS
RET_CHUNK = 128
ROPE_BASE = 10000.0
POOL_WINDOWS = (2, 4, 8, 16)
POOL_GROUP = POOL_WIDTH // len(POOL_WINDOWS)
IN_WIDTH = 4 * RET_WIDTH + POOL_WIDTH
N_EXPERTS = 32
TOP_K = 4
D_FF = D_MODEL
SWIGLU_ALPHA = 1.702
SWIGLU_LIMIT = 7.0
MOE_BLOCK = 256
LN_EPS = 1e-5
DN_ALPHA = (2 * DEPTH) ** 0.25
DN_BETA = (8 * DEPTH) ** -0.25

kernel_name = 'hybrid_retention_pool_moe_encoder'


def layer_norm(x, g, b):
    xf = x.astype(jnp.float32)
    mu = jnp.mean(xf, -1, keepdims=True)
    xc = xf - mu
    var = jnp.mean(xc * xc, -1, keepdims=True)
    y = xc * lax.rsqrt(var + LN_EPS) * g.astype(jnp.float32) + b.astype(jnp.float32)
    return y.astype(x.dtype)


def rotary(t):
    L, d = t.shape[2], t.shape[3]
    inv_freq = ROPE_BASE ** (-jnp.arange(0, d, 2, dtype=jnp.float32) / d)
    ang = jnp.arange(L, dtype=jnp.float32)[:, None] * inv_freq[None, :]
    cos = jnp.cos(ang).astype(t.dtype)
    sin = jnp.sin(ang).astype(t.dtype)
    t1, t2 = t[..., : d // 2], t[..., d // 2:]
    return jnp.concatenate([t1 * cos - t2 * sin, t1 * sin + t2 * cos], axis=-1)


def retention_chunkwise(q, k, v, log_gamma, include_diag):
    B, H, L, d = q.shape
    C = RET_CHUNK
    nc = L // C
    dt = q.dtype
    qc = q.reshape(B, H, nc, C, d)
    kc = k.reshape(B, H, nc, C, d)
    vc = v.reshape(B, H, nc, C, d)
    idx = jnp.arange(C, dtype=jnp.float32)
    rel = idx[:, None] - idx[None, :]
    mask = (rel >= 0) if include_diag else (rel > 0)
    lg = log_gamma[:, None, None]
    decay = jnp.exp(jnp.where(mask[None], lg * rel[None], -jnp.inf)).astype(dt)
    scores = jnp.einsum('bhnid,bhnjd->bhnij', qc, kc) * decay[None, :, None]
    o_intra = jnp.einsum('bhnij,bhnjd->bhnid', scores, vc)
    zeta = jnp.exp(log_gamma[:, None] * (C - 1 - idx)[None, :]).astype(dt)
    xi = jnp.exp(log_gamma[:, None] * (idx + 1)[None, :]).astype(dt)
    chunk_decay = jnp.exp(log_gamma * C).astype(dt)[None, :, None, None]
    kv = jnp.einsum('bhncd,bhnce->nbhde', kc * zeta[None, :, None, :, None], vc)

    def step(state, kv_n):
        return state * chunk_decay + kv_n, state

    _, prev = lax.scan(step, jnp.zeros_like(kv[0]), kv)
    o_cross = jnp.einsum('bhncd,nbhde->bhnce', qc * xi[None, :, None, :, None], prev)
    return (o_intra + o_cross).reshape(B, H, L, d)


def retention_group(pq, pk, pv, pg, decay_logit_fwd, decay_logit_bwd):
    B, L, _ = pq.shape

    def heads(t):
        return t.reshape(B, L, N_RET_HEADS, RET_HEAD_DIM).transpose(0, 2, 1, 3)

    q = rotary(heads(pq))
    k = rotary(heads(pk)) * (RET_HEAD_DIM ** -0.5)
    v = heads(pv)
    lg_f = jax.nn.log_sigmoid(decay_logit_fwd.astype(jnp.float32))
    lg_b = jax.nn.log_sigmoid(decay_logit_bwd.astype(jnp.float32))
    o_f = retention_chunkwise(q, k, v, lg_f, True)
    o_b = jnp.flip(retention_chunkwise(jnp.flip(q, 2), jnp.flip(k, 2), jnp.flip(v, 2), lg_b, False), 2)
    o = (o_f + o_b).astype(jnp.float32)
    mu = jnp.mean(o, -1, keepdims=True)
    oc = o - mu
    o = (oc * lax.rsqrt(jnp.mean(oc * oc, -1, keepdims=True) + LN_EPS)).astype(pq.dtype)
    o = o.transpose(0, 2, 1, 3).reshape(B, L, RET_WIDTH)
    return jax.nn.silu(pg) * o


def pool_group(u, w_pool, pool_scale):
    B, L, _ = u.shape
    uf = u.astype(jnp.float32)
    cs = jnp.concatenate([jnp.zeros((B, 1, POOL_WIDTH), jnp.float32), jnp.cumsum(uf, axis=1)], axis=1)
    pos = jnp.arange(L)
    groups = []
    for gi, w in enumerate(POOL_WINDOWS):
        sl = slice(gi * POOL_GROUP, (gi + 1) * POOL_GROUP)
        lo = jnp.clip(pos - w // 2, 0, L - 1)
        hi = jnp.clip(pos + (w - 1 - w // 2), 0, L - 1)
        cg = cs[..., sl]
        mean = (cg[:, hi + 1] - cg[:, lo]) / (hi - lo + 1).astype(jnp.float32)[None, :, None]
        groups.append(mean - uf[..., sl])
    pooled = jnp.stack(groups, axis=2).astype(u.dtype)
    mixed = jnp.einsum('blgc,gcd->blgd', pooled, w_pool)
    return mixed.reshape(B, L, POOL_WIDTH) * pool_scale


def token_mixer(h, w_in, decay_logit_fwd, decay_logit_bwd, w_pool, pool_scale, w_o):
    proj = h @ w_in
    R = RET_WIDTH
    pq, pk, pv, pg, pu = jnp.split(proj, [R, 2 * R, 3 * R, 4 * R], axis=-1)
    ret = retention_group(pq, pk, pv, pg, decay_logit_fwd, decay_logit_bwd)
    pool = pool_group(pu, w_pool, pool_scale)
    return jnp.concatenate([ret, pool], axis=-1) @ w_o


def moe(h, w_router, b_router, w_gate, b_gate, w_up, b_up, w_down, b_down):
    B, L, D = h.shape
    N = B * L
    NK = N * TOP_K
    hf = h.reshape(N, D)
    logits = (hf @ w_router + b_router).astype(jnp.float32)
    top_vals, top_idx = lax.top_k(logits, TOP_K)
    gates = jax.nn.softmax(top_vals, axis=-1).astype(hf.dtype)
    e_flat = top_idx.reshape(-1).astype(jnp.int32)
    t_flat = jnp.arange(NK, dtype=jnp.int32) // TOP_K
    g_flat = gates.reshape(-1)
    order = jnp.argsort(e_flat)
    e_sorted = e_flat[order]
    counts = jnp.bincount(e_flat, length=N_EXPERTS).astype(jnp.int32)
    padded = (counts + MOE_BLOCK - 1) // MOE_BLOCK * MOE_BLOCK
    start_u = jnp.cumsum(counts) - counts
    ends_p = jnp.cumsum(padded)
    start_p = ends_p - padded
    dest = start_p[e_sorted] + jnp.arange(NK, dtype=jnp.int32) - start_u[e_sorted]
    n_blocks = -(-NK // MOE_BLOCK) + N_EXPERTS
    P = n_blocks * MOE_BLOCK
    tok_buf = jnp.full((P,), N, jnp.int32).at[dest].set(t_flat[order])
    gate_buf = jnp.zeros((P,), hf.dtype).at[dest].set(g_flat[order])
    block_e = jnp.searchsorted(ends_p, jnp.arange(n_blocks, dtype=jnp.int32) * MOE_BLOCK, side='right')
    block_e = jnp.minimum(block_e, N_EXPERTS - 1).astype(jnp.int32)
    hpad = jnp.concatenate([hf, jnp.zeros((1, D), hf.dtype)], axis=0)

    def body(acc, inp):
        tok, gw, e = inp
        xb = hpad[tok]
        gt = xb @ w_gate[e] + b_gate[e]
        up = xb @ w_up[e] + b_up[e]
        gt = jnp.minimum(gt, SWIGLU_LIMIT)
        up = jnp.clip(up, -SWIGLU_LIMIT, SWIGLU_LIMIT)
        hid = (up + 1) * (gt * jax.nn.sigmoid(SWIGLU_ALPHA * gt))
        yb = hid @ w_down[e] + b_down[e]
        return acc.at[tok].add(yb * gw[:, None]), None

    acc0 = jnp.zeros((N + 1, D), hf.dtype)
    acc, _ = lax.scan(body, acc0, (tok_buf.reshape(n_blocks, MOE_BLOCK), gate_buf.reshape(n_blocks, MOE_BLOCK), block_e))
    return acc[:N].reshape(B, L, D)


def encoder_trunk(x, ln_in_g, ln_in_b, w_in, decay_logit_fwd, decay_logit_bwd, w_pool, pool_scale, w_o,
                  ln_mix_g, ln_mix_b, w_router, b_router, w_gate, b_gate, w_up, b_up, w_down, b_down,
                  ln_ffn_g, ln_ffn_b):
    h = layer_norm(x, ln_in_g, ln_in_b)
    for l in range(DEPTH):
        m = token_mixer(h, w_in[l], decay_logit_fwd[l], decay_logit_bwd[l], w_pool[l], pool_scale[l], w_o[l])
        h = layer_norm(DN_ALPHA * h + m, ln_mix_g[l], ln_mix_b[l])
        f = moe(h, w_router[l], b_router[l], w_gate[l], b_gate[l], w_up[l], b_up[l], w_down[l], b_down[l])
        h = layer_norm(DN_ALPHA * h + f, ln_ffn_g[l], ln_ffn_b[l])
    return h


def setup_inputs(seed: int = 0) -> dict:
    key = jax.random.key(seed)
    ks = jax.random.split(key, 24)
    f32 = jnp.float32

    def nrm(k, shape, s):
        return jax.random.normal(k, shape, f32) * s

    D = D_MODEL
    x_prompt = nrm(ks[0], (BATCH, SEQ, D), 1.0)
    x_sample = nrm(ks[1], (DEC_BATCH, DEC_SEQ, D), 1.0)
    ln_in_g = 1.0 + nrm(ks[2], (D,), 0.02)
    ln_in_b = nrm(ks[3], (D,), 0.02)
    col_scale = jnp.ones((IN_WIDTH,), f32).at[2 * RET_WIDTH:3 * RET_WIDTH].set(DN_BETA)
    w_in = nrm(ks[4], (DEPTH, D, IN_WIDTH), D ** -0.5) * col_scale
    base = jnp.log(2.0 ** (5.0 + jnp.arange(N_RET_HEADS, dtype=f32)) - 1.0)
    decay_logit_fwd = base[None] + nrm(ks[5], (DEPTH, N_RET_HEADS), 0.05)
    decay_logit_bwd = base[None] + nrm(ks[6], (DEPTH, N_RET_HEADS), 0.05)
    w_pool = nrm(ks[7], (DEPTH, len(POOL_WINDOWS), POOL_GROUP, POOL_GROUP), POOL_GROUP ** -0.5)
    pool_scale = 1.0 + nrm(ks[8], (DEPTH, POOL_WIDTH), 0.02)
    w_o = nrm(ks[9], (DEPTH, MIX_WIDTH, D), MIX_WIDTH ** -0.5 * DN_BETA)
    ln_mix_g = 1.0 + nrm(ks[10], (DEPTH, D), 0.02)
    ln_mix_b = nrm(ks[11], (DEPTH, D), 0.02)
    w_router = nrm(ks[12], (DEPTH, D, N_EXPERTS), D ** -0.5)
    b_router = nrm(ks[13], (DEPTH, N_EXPERTS), 0.01)
    w_gate = nrm(ks[14], (DEPTH, N_EXPERTS, D, D_FF), D ** -0.5)
    b_gate = nrm(ks[15], (DEPTH, N_EXPERTS, D_FF), 0.02)
    w_up = nrm(ks[16], (DEPTH, N_EXPERTS, D, D_FF), D ** -0.5)
    b_up = nrm(ks[17], (DEPTH, N_EXPERTS, D_FF), 0.02)
    w_down = nrm(ks[18], (DEPTH, N_EXPERTS, D_FF, D), D_FF ** -0.5 * DN_BETA)
    b_down = nrm(ks[19], (DEPTH, N_EXPERTS, D), 0.02)
    ln_ffn_g = 1.0 + nrm(ks[20], (DEPTH, D), 0.02)
    ln_ffn_b = nrm(ks[21], (DEPTH, D), 0.02)
    return {'x_prompt': x_prompt, 'x_sample': x_sample, 'ln_in_g': ln_in_g, 'ln_in_b': ln_in_b,
            'w_in': w_in, 'decay_logit_fwd': decay_logit_fwd, 'decay_logit_bwd': decay_logit_bwd,
            'w_pool': w_pool, 'pool_scale': pool_scale, 'w_o': w_o, 'ln_mix_g': ln_mix_g, 'ln_mix_b': ln_mix_b,
            'w_router': w_router, 'b_router': b_router, 'w_gate': w_gate, 'b_gate': b_gate,
            'w_up': w_up, 'b_up': b_up, 'w_down': w_down, 'b_down': b_down,
            'ln_ffn_g': ln_ffn_g, 'ln_ffn_b': ln_ffn_b}


def reference(x_prompt, x_sample, ln_in_g, ln_in_b, w_in, decay_logit_fwd, decay_logit_bwd, w_pool, pool_scale,
              w_o, ln_mix_g, ln_mix_b, w_router, b_router, w_gate, b_gate, w_up, b_up, w_down, b_down,
              ln_ffn_g, ln_ffn_b):
    y_prompt = encoder_trunk(x_prompt, ln_in_g, ln_in_b, w_in, decay_logit_fwd, decay_logit_bwd, w_pool, pool_scale,
                             w_o, ln_mix_g, ln_mix_b, w_router, b_router, w_gate, b_gate, w_up, b_up, w_down, b_down,
                             ln_ffn_g, ln_ffn_b)
    y_sample = encoder_trunk(x_sample, ln_in_g, ln_in_b, w_in, decay_logit_fwd, decay_logit_bwd, w_pool, pool_scale,
                             w_o, ln_mix_g, ln_mix_b, w_router, b_router, w_gate, b_gate, w_up, b_up, w_down, b_down,
                             ln_ffn_g, ln_ffn_b)
    return (y_prompt, y_sample)
```

```python
import functools

import jax
import jax.numpy as jnp
import numpy as np
from jax import lax
from jax.experimental import pallas as pl
from jax.experimental.pallas import tpu as pltpu

D_MODEL = 1024
RET_WIDTH = 512
POOL_WIDTH = 512
N_HEADS = 4
HEAD_DIM = 128
ROPE_BASE = 10000.0
POOL_WINDOWS = (2, 4, 8, 16)
POOL_GROUP = 128
IN_WIDTH = 4 * RET_WIDTH + POOL_WIDTH
N_EXPERTS = 32
TOP_K = 4
SWIGLU_ALPHA = 1.702
SWIGLU_LIMIT = 7.0
LN_EPS = 1e-5
DN_ALPHA = 2.0 ** 0.25

TM_PROJ = 512
RET_TILE = 1024
RET_CHUNK = 256
TM_MIX = 512
POOL_HALO = 16
TM_DISPATCH = 256
TM_EXPERT = 512
TM_COMBINE = 256

VMEM_LIMIT = 56 * 1024 * 1024

_NT = (((1,), (1,)), ((), ()))
_TN = (((0,), (0,)), ((), ()))


def _layer_norm(x, g, b):
    mu = jnp.mean(x, axis=-1, keepdims=True)
    xc = x - mu
    var = jnp.mean(xc * xc, axis=-1, keepdims=True)
    return xc * lax.rsqrt(var + LN_EPS) * g + b


def _bf16(x):
    return x.astype(jnp.bfloat16)


def _dot(a, b):
    return jnp.dot(a, b, preferred_element_type=jnp.float32)


def _ln_proj_kernel(x_ref, g_ref, b_ref, w_ref, cos_ref, sin_ref,
                    q_ref, k_ref, v_ref, sg_ref, u_ref):
    h = _bf16(_layer_norm(x_ref[...], g_ref[...], b_ref[...]))
    cos = cos_ref[...]
    sin = sin_ref[...]

    def rotary(t):
        return t * cos + pltpu.roll(t, HEAD_DIM // 2, axis=1) * sin

    R = RET_WIDTH
    pq = _dot(h, w_ref[:, 0:R])
    for hd in range(N_HEADS):
        sl = slice(hd * HEAD_DIM, (hd + 1) * HEAD_DIM)
        q_ref[:, sl] = _bf16(rotary(pq[:, sl]))
    pk = _dot(h, w_ref[:, R:2 * R])
    for hd in range(N_HEADS):
        sl = slice(hd * HEAD_DIM, (hd + 1) * HEAD_DIM)
        k_ref[:, sl] = _bf16(rotary(pk[:, sl]) * (HEAD_DIM ** -0.5))
    v_ref[...] = _bf16(_dot(h, w_ref[:, 2 * R:3 * R]))
    pg = _dot(h, w_ref[:, 3 * R:4 * R])
    sg_ref[...] = _bf16(pg * jax.nn.sigmoid(pg))
    u_ref[...] = _bf16(_dot(h, w_ref[:, 4 * R:]))


def _ln_proj(x2, ln_g, ln_b, w_in_bf, cos_t, sin_t, seq_len):
    n = x2.shape[0]
    tm = TM_PROJ
    tiles_per_seq = seq_len // tm
    row = lambda i: (i, 0)
    const = lambda i: (0, 0)
    out_sd = jax.ShapeDtypeStruct((n, RET_WIDTH), jnp.bfloat16)
    return pl.pallas_call(
        _ln_proj_kernel,
        grid=(n // tm,),
        in_specs=[
            pl.BlockSpec((tm, D_MODEL), row),
            pl.BlockSpec((1, D_MODEL), const),
            pl.BlockSpec((1, D_MODEL), const),
            pl.BlockSpec((D_MODEL, IN_WIDTH), const),
            pl.BlockSpec((tm, HEAD_DIM), lambda i: (i % tiles_per_seq, 0)),
            pl.BlockSpec((tm, HEAD_DIM), lambda i: (i % tiles_per_seq, 0)),
        ],
        out_specs=[pl.BlockSpec((tm, RET_WIDTH), row)] * 5,
        out_shape=[out_sd] * 5,
        compiler_params=pltpu.CompilerParams(
            dimension_semantics=("arbitrary",), vmem_limit_bytes=VMEM_LIMIT),
        name="ln_proj",
    )(x2, ln_g, ln_b, w_in_bf, cos_t, sin_t)


def _log_sigmoid(x):
    return jnp.minimum(x, 0.0) - jnp.log(1.0 + jnp.exp(-jnp.abs(x)))


def _retention_kernel(dlf_ref, dlb_ref, q_ref, k_ref, v_ref, sg_ref, o_ref,
                      dmat, xi_f, xi_b, zeta_f, zeta_b, cdec, s_f, s_b, snap):
    b = pl.program_id(0)
    phase = pl.program_id(1)
    j = pl.program_id(2)
    n_tiles = pl.num_programs(2)
    C = RET_CHUNK
    sub = RET_TILE // C

    @pl.when((b == 0) & (phase == 0) & (j == 0))
    def _init_tables():
        ri = lax.broadcasted_iota(jnp.int32, (C, C), 0).astype(jnp.float32)
        ci = lax.broadcasted_iota(jnp.int32, (C, C), 1).astype(jnp.float32)
        rel = ri - ci
        pos = lax.broadcasted_iota(jnp.int32, (C, HEAD_DIM), 0).astype(jnp.float32)
        zero_cc = jnp.zeros((C, C), jnp.float32)
        zero_cd = jnp.zeros((C, HEAD_DIM), jnp.float32)
        zero_dd = jnp.zeros((HEAD_DIM, HEAD_DIM), jnp.float32)
        for hd in range(N_HEADS):
            dlf = dlf_ref[hd]
            dlb = dlb_ref[hd]
            dmat[hd] = jnp.where(rel >= 0, jnp.exp(_log_sigmoid(zero_cc + dlf) * rel),
                                 jnp.exp(-_log_sigmoid(zero_cc + dlb) * rel))
            lf = _log_sigmoid(zero_cd + dlf)
            lb = _log_sigmoid(zero_cd + dlb)
            xi_f[hd] = jnp.exp(lf * (pos + 1.0))
            xi_b[hd] = jnp.exp(lb * (C - pos))
            zeta_f[hd] = jnp.exp(lf * (C - 1.0 - pos))
            zeta_b[hd] = jnp.exp(lb * pos)
            cdec[hd, 0] = jnp.exp(_log_sigmoid(zero_dd + dlf) * float(C))
            cdec[hd, 1] = jnp.exp(_log_sigmoid(zero_dd + dlb) * float(C))

    @pl.when(j == 0)
    def _reset_state():
        @pl.when(phase == 0)
        def _():
            s_b[...] = jnp.zeros_like(s_b)

        @pl.when(phase == 1)
        def _():
            s_f[...] = jnp.zeros_like(s_f)

    @pl.when(phase == 0)
    def _backward_states():
        tile = n_tiles - 1 - j

        def body(cc, carry):
            c = sub - 1 - cc
            r0 = pl.multiple_of(c * C, C)
            chunk = tile * sub + c
            for hd in range(N_HEADS):
                sl = slice(hd * HEAD_DIM, (hd + 1) * HEAD_DIM)
                kc = k_ref[0, pl.ds(r0, C), sl]
                vc = v_ref[0, pl.ds(r0, C), sl].astype(jnp.float32)
                st = s_b[hd]
                snap[chunk * N_HEADS + hd] = _bf16(st)
                kv = lax.dot_general(kc, _bf16(vc * zeta_b[hd]), _TN,
                                     preferred_element_type=jnp.float32)
                s_b[hd] = st * cdec[hd, 1] + kv
            return carry

        lax.fori_loop(0, sub, body, 0)

    @pl.when(phase == 1)
    def _forward_outputs():
        def body(c, carry):
            r0 = pl.multiple_of(c * C, C)
            chunk = j * sub + c
            for hd in range(N_HEADS):
                sl = slice(hd * HEAD_DIM, (hd + 1) * HEAD_DIM)
                qc = q_ref[0, pl.ds(r0, C), sl]
                kc = k_ref[0, pl.ds(r0, C), sl]
                vc = v_ref[0, pl.ds(r0, C), sl]
                scores = lax.dot_general(qc, kc, _NT, preferred_element_type=jnp.float32)
                o = _dot(_bf16(scores * dmat[hd]), vc)
                st = s_f[hd]
                o = o + xi_f[hd] * _dot(qc, _bf16(st))
                o = o + xi_b[hd] * _dot(qc, snap[chunk * N_HEADS + hd])
                kv = lax.dot_general(kc, _bf16(vc.astype(jnp.float32) * zeta_f[hd]), _TN,
                                     preferred_element_type=jnp.float32)
                s_f[hd] = st * cdec[hd, 0] + kv
                mu = jnp.mean(o, axis=-1, keepdims=True)
                oc = o - mu
                on = oc * lax.rsqrt(jnp.mean(oc * oc, axis=-1, keepdims=True) + LN_EPS)
                sg = sg_ref[0, pl.ds(r0, C), sl].astype(jnp.float32)
                o_ref[0, pl.ds(r0, C), sl] = _bf16(sg * on)
            return carry

        lax.fori_loop(0, sub, body, 0)


def _retention(q, k, v, sg, dl_f, dl_b):
    bsz, seq_len, _ = q.shape
    n_tiles = seq_len // RET_TILE
    n_chunks = seq_len // RET_CHUNK
    C = RET_CHUNK

    def kv_map(b, p, j):
        return (b, jnp.where(p == 0, n_tiles - 1 - j, j), 0)

    def q_map(b, p, j):
        return (b, jnp.where(p == 0, 0, j), 0)

    blk = (1, RET_TILE, RET_WIDTH)
    smem = pl.BlockSpec(memory_space=pltpu.SMEM)
    return pl.pallas_call(
        _retention_kernel,
        grid=(bsz, 2, n_tiles),
        in_specs=[smem, smem,
                  pl.BlockSpec(blk, q_map), pl.BlockSpec(blk, kv_map),
                  pl.BlockSpec(blk, kv_map), pl.BlockSpec(blk, q_map)],
        out_specs=pl.BlockSpec(blk, q_map),
        out_shape=jax.ShapeDtypeStruct(q.shape, jnp.bfloat16),
        scratch_shapes=[
            pltpu.VMEM((N_HEADS, C, C), jnp.float32),
            pltpu.VMEM((N_HEADS, C, HEAD_DIM), jnp.float32),
            pltpu.VMEM((N_HEADS, C, HEAD_DIM), jnp.float32),
            pltpu.VMEM((N_HEADS, C, HEAD_DIM), jnp.float32),
            pltpu.VMEM((N_HEADS, C, HEAD_DIM), jnp.float32),
            pltpu.VMEM((N_HEADS, 2, HEAD_DIM, HEAD_DIM), jnp.float32),
            pltpu.VMEM((N_HEADS, HEAD_DIM, HEAD_DIM), jnp.float32),
            pltpu.VMEM((N_HEADS, HEAD_DIM, HEAD_DIM), jnp.float32),
            pltpu.VMEM((n_chunks * N_HEADS, HEAD_DIM, HEAD_DIM), jnp.bfloat16),
        ],
        compiler_params=pltpu.CompilerParams(
            dimension_semantics=("arbitrary", "arbitrary", "arbitrary"),
            vmem_limit_bytes=VMEM_LIMIT),
        name="retention",
    )(dl_f, dl_b, q, k, v, sg)


def _mix_router_kernel(seq_len, x_ref, ret_ref, u_ref, up_ref, un_ref,
                       lng_ref, lnb_ref, wpool_ref, pscale_ref, wo_ref,
                       mg_ref, mb_ref, wr_hi_ref, wr_lo_ref, br_ref,
                       h2_ref, idx_ref, rank_ref, gate_ref, cnt_ref,
                       band, tri, base):
    i = pl.program_id(0)
    tm = TM_MIX
    halo = POOL_HALO
    tiles_per_seq = seq_len // tm
    ts = i % tiles_per_seq

    @pl.when(i == 0)
    def _init():
        r = lax.broadcasted_iota(jnp.int32, (tm, tm + 2 * halo), 0)
        c = lax.broadcasted_iota(jnp.int32, (tm, tm + 2 * halo), 1) - halo
        for gi, w in enumerate(POOL_WINDOWS):
            inside = (c >= r - w // 2) & (c <= r + (w - 1 - w // 2))
            band[gi] = jnp.where(inside, 1.0, 0.0).astype(jnp.bfloat16)
        tr = lax.broadcasted_iota(jnp.int32, (tm, tm), 0)
        tc = lax.broadcasted_iota(jnp.int32, (tm, tm), 1)
        tri[...] = jnp.where(tr < tc, 1.0, 0.0).astype(jnp.bfloat16)
        base[...] = jnp.zeros_like(base)

    prev = jnp.where(ts == 0, jnp.zeros_like(up_ref[...]), up_ref[...])
    nxt = jnp.where(ts == tiles_per_seq - 1, jnp.zeros_like(un_ref[...]), un_ref[...])
    u_cur = u_ref[...]
    u_ext = jnp.concatenate([prev, u_cur, nxt], axis=0)
    pos = ts * tm + lax.broadcasted_iota(jnp.int32, (tm, POOL_GROUP), 0)
    mixed = []
    for gi, w in enumerate(POOL_WINDOWS):
        sl = slice(gi * POOL_GROUP, (gi + 1) * POOL_GROUP)
        lo = jnp.maximum(pos - w // 2, 0)
        hi = jnp.minimum(pos + (w - 1 - w // 2), seq_len - 1)
        cnt = (hi - lo + 1).astype(jnp.float32)
        wsum = _dot(band[gi], u_ext[:, sl])
        pooled = wsum / cnt - u_cur[:, sl].astype(jnp.float32)
        mx = _dot(_bf16(pooled), wpool_ref[gi]) * pscale_ref[:, sl]
        mixed.append(_bf16(mx))
    pool = jnp.concatenate(mixed, axis=1)

    m = _dot(ret_ref[...], wo_ref[0:RET_WIDTH, :]) + _dot(pool, wo_ref[RET_WIDTH:, :])
    h = _layer_norm(x_ref[...], lng_ref[...], lnb_ref[...])
    h2 = _layer_norm(DN_ALPHA * h + m, mg_ref[...], mb_ref[...])
    h2_ref[...] = h2

    h_hi = _bf16(h2)
    h_lo = _bf16(h2 - h_hi.astype(jnp.float32))
    wr_hi = wr_hi_ref[...]
    logits = (lax.dot_general(wr_hi, h_hi, _NT, preferred_element_type=jnp.float32)
              + lax.dot_general(wr_hi, h_lo, _NT, preferred_element_type=jnp.float32)
              + lax.dot_general(wr_lo_ref[...], h_hi, _NT, preferred_element_type=jnp.float32)
              + br_ref[...])

    e_iota = lax.broadcasted_iota(jnp.int32, (N_EXPERTS, tm), 0)
    k_iota = lax.broadcasted_iota(jnp.int32, (8, tm), 0)
    vals8 = jnp.zeros((8, tm), jnp.float32)
    idx8 = jnp.zeros((8, tm), jnp.int32)
    sels = []
    work = logits
    for kk in range(TOP_K):
        mval = jnp.max(work, axis=0, keepdims=True)
        midx = jnp.min(jnp.where(work == mval, e_iota, N_EXPERTS), axis=0, keepdims=True)
        sel = e_iota == midx
        sels.append(sel)
        vals8 = jnp.where(k_iota == kk, mval, vals8)
        idx8 = jnp.where(k_iota == kk, midx, idx8)
        work = jnp.where(sel, -jnp.inf, work)

    ex = jnp.where(k_iota < TOP_K, jnp.exp(vals8 - vals8[0:1, :]), 0.0)
    gates8 = ex / jnp.sum(ex, axis=0, keepdims=True)

    onehot = jnp.zeros((N_EXPERTS, tm), jnp.float32)
    for sel in sels:
        onehot = onehot + jnp.where(sel, 1.0, 0.0)
    cum = _dot(_bf16(onehot), tri[...])
    before = base[...] + cum
    rank8 = jnp.zeros((8, tm), jnp.int32)
    for kk, sel in enumerate(sels):
        rk = jnp.sum(jnp.where(sel, before, 0.0), axis=0, keepdims=True).astype(jnp.int32)
        rank8 = jnp.where(k_iota == kk, rk, rank8)
    total = cum[:, tm - 1:tm] + onehot[:, tm - 1:tm]
    new_base = base[...] + jnp.broadcast_to(total, (N_EXPERTS, tm))
    base[...] = new_base

    idx_ref[...] = idx8
    rank_ref[...] = rank8
    gpad = jnp.concatenate([gates8, jnp.zeros((128 - 8, tm), jnp.float32)], axis=0)
    gate_ref[...] = gpad.T
    cnt_ref[...] = new_base[:, 0:128].astype(jnp.int32)


def _mix_router(x2, ret, u, ln_g, ln_b, w_pool_bf, pool_scale, w_o_bf, mix_g, mix_b,
                wr_hi, wr_lo, br_full, seq_len):
    n = x2.shape[0]
    tm = TM_MIX
    halo = POOL_HALO
    hb = tm // halo
    n_halo_blocks = n // halo
    row = lambda i: (i, 0)
    const = lambda i: (0, 0)
    col = lambda i: (0, i)
    return pl.pallas_call(
        functools.partial(_mix_router_kernel, seq_len),
        grid=(n // tm,),
        in_specs=[
            pl.BlockSpec((tm, D_MODEL), row),
            pl.BlockSpec((tm, RET_WIDTH), row),
            pl.BlockSpec((tm, POOL_WIDTH), row),
            pl.BlockSpec((halo, POOL_WIDTH), lambda i: (jnp.maximum(i * hb - 1, 0), 0)),
            pl.BlockSpec((halo, POOL_WIDTH),
                         lambda i: (jnp.minimum((i + 1) * hb, n_halo_blocks - 1), 0)),
            pl.BlockSpec((1, D_MODEL), const),
            pl.BlockSpec((1, D_MODEL), const),
            pl.BlockSpec((len(POOL_WINDOWS), POOL_GROUP, POOL_GROUP), lambda i: (0, 0, 0)),
            pl.BlockSpec((1, POOL_WIDTH), const),
            pl.BlockSpec((D_MODEL, D_MODEL), const),
            pl.BlockSpec((1, D_MODEL), const),
            pl.BlockSpec((1, D_MODEL), const),
            pl.BlockSpec((N_EXPERTS, D_MODEL), const),
            pl.BlockSpec((N_EXPERTS, D_MODEL), const),
            pl.BlockSpec((N_EXPERTS, tm), const),
        ],
        out_specs=[
            pl.BlockSpec((tm, D_MODEL), row),
            pl.BlockSpec((8, tm), col),
            pl.BlockSpec((8, tm), col),
            pl.BlockSpec((tm, 128), row),
            pl.BlockSpec((N_EXPERTS, 128), const),
        ],
        out_shape=[
            jax.ShapeDtypeStruct((n, D_MODEL), jnp.float32),
            jax.ShapeDtypeStruct((8, n), jnp.int32),
            jax.ShapeDtypeStruct((8, n), jnp.int32),
            jax.ShapeDtypeStruct((n, 128), jnp.float32),
            jax.ShapeDtypeStruct((N_EXPERTS, 128), jnp.int32),
        ],
        scratch_shapes=[
            pltpu.VMEM((len(POOL_WINDOWS), tm, tm + 2 * halo), jnp.bfloat16),
            pltpu.VMEM((tm, tm), jnp.bfloat16),
            pltpu.VMEM((N_EXPERTS, tm), jnp.float32),
        ],
        compiler_params=pltpu.CompilerParams(
            dimension_semantics=("arbitrary",), vmem_limit_bytes=VMEM_LIMIT),
        name="mix_router",
    )(x2, ret, u, u, u, ln_g, ln_b, w_pool_bf, pool_scale, w_o_bf, mix_g, mix_b,
      wr_hi, wr_lo, br_full)


def _dispatch_kernel(slot_ref, h2_ref, xs_ref, sem):
    tm = TM_DISPATCH

    def row_copy(r, kk):
        s = slot_ref[kk, r]
        return pltpu.make_async_copy(h2_ref.at[pl.ds(r, 1)], xs_ref.at[pl.ds(s, 1)], sem)

    def start(r, carry):
        for kk in range(TOP_K):
            row_copy(r, kk).start()
        return carry

    def wait(r, carry):
        for kk in range(TOP_K):
            row_copy(r, kk).wait()
        return carry

    lax.fori_loop(0, tm, start, 0)
    lax.fori_loop(0, tm, wait, 0)


def _dispatch(slots, h2):
    n = h2.shape[0]
    tm = TM_DISPATCH
    return pl.pallas_call(
        _dispatch_kernel,
        grid=(n // tm,),
        in_specs=[
            pl.BlockSpec((TOP_K, tm), lambda i: (0, i), memory_space=pltpu.SMEM),
            pl.BlockSpec((tm, D_MODEL), lambda i: (i, 0)),
        ],
        out_specs=pl.BlockSpec(memory_space=pl.ANY),
        out_shape=jax.ShapeDtypeStruct((n * TOP_K, D_MODEL), jnp.float32),
        scratch_shapes=[pltpu.SemaphoreType.DMA(())],
        compiler_params=pltpu.CompilerParams(
            dimension_semantics=("arbitrary",), vmem_limit_bytes=VMEM_LIMIT),
        name="dispatch",
    )(slots, h2)


def _expert_kernel(it_tile, it_exp, it_lo, it_hi, xs_ref, wg_ref, bg_ref, wu_ref, bu_ref,
                   wd_ref, bd_ref, ys_ref):
    i = pl.program_id(0)
    lo = it_lo[i]
    hi = it_hi[i]

    @pl.when(hi > lo)
    def _():
        x = _bf16(xs_ref[...])
        gt = _dot(x, wg_ref[0]) + bg_ref[0]
        up = _dot(x, wu_ref[0]) + bu_ref[0]
        gt = jnp.minimum(gt, SWIGLU_LIMIT)
        up = jnp.clip(up, -SWIGLU_LIMIT, SWIGLU_LIMIT)
        hid = (up + 1.0) * (gt * jax.nn.sigmoid(SWIGLU_ALPHA * gt))
        y = _dot(_bf16(hid), wd_ref[0]) + bd_ref[0]
        r = lax.broadcasted_iota(jnp.int32, y.shape, 0)
        mine = (r >= lo) & (r < hi)

        @pl.when(lo == 0)
        def _():
            ys_ref[...] = jnp.where(mine, y, 0.0)

        @pl.when(lo > 0)
        def _():
            ys_ref[...] = jnp.where(mine, y, ys_ref[...])


def _experts(items, xs, w_gate_bf, b_gate, w_up_bf, b_up, w_down_bf, b_down):
    it_tile, it_exp, it_lo, it_hi = items
    p = xs.shape[0]
    tm = TM_EXPERT
    n_items = it_tile.shape[0]
    tile_map = lambda i, t, e, lo, hi: (t[i], 0)
    w_map = lambda i, t, e, lo, hi: (e[i], 0, 0)
    w_spec = pl.BlockSpec((1, D_MODEL, D_MODEL), w_map)
    b_spec = pl.BlockSpec((1, 1, D_MODEL), w_map)
    grid_spec = pltpu.PrefetchScalarGridSpec(
        num_scalar_prefetch=4,
        grid=(n_items,),
        in_specs=[pl.BlockSpec((tm, D_MODEL), tile_map),
                  w_spec, b_spec, w_spec, b_spec, w_spec, b_spec],
        out_specs=pl.BlockSpec((tm, D_MODEL), tile_map),
    )
    return pl.pallas_call(
        _expert_kernel,
        grid_spec=grid_spec,
        out_shape=jax.ShapeDtypeStruct((p, D_MODEL), jnp.float32),
        compiler_params=pltpu.CompilerParams(
            dimension_semantics=("arbitrary",), vmem_limit_bytes=VMEM_LIMIT),
        name="experts",
    )(it_tile, it_exp, it_lo, it_hi, xs, w_gate_bf, b_gate, w_up_bf, b_up, w_down_bf, b_down)


def _expert_items(counts, n_rows):
    tm = TM_EXPERT
    n_tiles = n_rows // tm
    n_items = n_tiles + N_EXPERTS - 1
    ends = jnp.cumsum(counts)
    starts = ends - counts
    first_tile = starts // tm
    last_tile = jnp.maximum(ends - 1, 0) // tm
    per_exp = jnp.where(counts > 0, last_tile - first_tile + 1, 0)
    item_end = jnp.cumsum(per_exp)
    item_start = item_end - per_exp
    total = item_end[-1]
    i = jnp.arange(n_items, dtype=jnp.int32)
    ic = jnp.minimum(i, total - 1)
    e = jnp.sum((ic[:, None] >= item_end[None, :]).astype(jnp.int32), axis=1)
    e = jnp.minimum(e, N_EXPERTS - 1)
    onehot = (e[:, None] == jnp.arange(N_EXPERTS, dtype=jnp.int32)[None, :]).astype(jnp.int32)
    pick = lambda a: jnp.sum(onehot * a[None, :], axis=1)
    tile = pick(first_tile) + ic - pick(item_start)
    lo = jnp.maximum(pick(starts), tile * tm) - tile * tm
    hi = jnp.minimum(pick(ends), (tile + 1) * tm) - tile * tm
    valid = i < total
    lo = jnp.where(valid, lo, 0)
    hi = jnp.where(valid, hi, 0)
    return (tile.astype(jnp.int32), e.astype(jnp.int32), lo.astype(jnp.int32),
            hi.astype(jnp.int32))


def _combine_kernel(slot_ref, h2_ref, gate_ref, g_ref, b_ref, ys_ref, o_ref, buf, sem):
    tm = TM_COMBINE

    def row_copy(r, kk):
        s = slot_ref[kk, r]
        return pltpu.make_async_copy(ys_ref.at[pl.ds(s, 1)], buf.at[kk, pl.ds(r, 1)], sem)

    def start(r, carry):
        for kk in range(TOP_K):
            row_copy(r, kk).start()
        return carry

    def wait(r, carry):
        for kk in range(TOP_K):
            row_copy(r, kk).wait()
        return carry

    lax.fori_loop(0, tm, start, 0)
    lax.fori_loop(0, tm, wait, 0)

    gates = gate_ref[...]
    f = gates[:, 0:1] * buf[0]
    for kk in range(1, TOP_K):
        f = f + gates[:, kk:kk + 1] * buf[kk]
    o_ref[...] = _layer_norm(DN_ALPHA * h2_ref[...] + f, g_ref[...], b_ref[...])


def _combine(slots, h2, gates, ffn_g, ffn_b, ys):
    n = h2.shape[0]
    tm = TM_COMBINE
    row = lambda i: (i, 0)
    const = lambda i: (0, 0)
    return pl.pallas_call(
        _combine_kernel,
        grid=(n // tm,),
        in_specs=[
            pl.BlockSpec((TOP_K, tm), lambda i: (0, i), memory_space=pltpu.SMEM),
            pl.BlockSpec((tm, D_MODEL), row),
            pl.BlockSpec((tm, 128), row),
            pl.BlockSpec((1, D_MODEL), const),
            pl.BlockSpec((1, D_MODEL), const),
            pl.BlockSpec(memory_space=pl.ANY),
        ],
        out_specs=pl.BlockSpec((tm, D_MODEL), row),
        out_shape=jax.ShapeDtypeStruct((n, D_MODEL), jnp.float32),
        scratch_shapes=[pltpu.VMEM((TOP_K, tm, D_MODEL), jnp.float32),
                        pltpu.SemaphoreType.DMA(())],
        compiler_params=pltpu.CompilerParams(
            dimension_semantics=("arbitrary",), vmem_limit_bytes=VMEM_LIMIT),
        name="combine",
    )(slots, h2, gates, ffn_g, ffn_b, ys)


def _rotary_tables(seq_len):
    d = HEAD_DIM
    inv_freq = ROPE_BASE ** (-jnp.arange(0, d, 2, dtype=jnp.float32) / d)
    ang = jnp.arange(seq_len, dtype=jnp.float32)[:, None] * inv_freq[None, :]
    cos = jnp.cos(ang)
    sin = jnp.sin(ang)
    return jnp.concatenate([cos, cos], axis=1), jnp.concatenate([-sin, sin], axis=1)


def _trunk(x, p):
    bsz, seq_len, _ = x.shape
    n = bsz * seq_len
    x2 = x.reshape(n, D_MODEL)
    cos_t, sin_t = _rotary_tables(seq_len)

    q, k, v, sg, u = _ln_proj(x2, p["ln_in_g"], p["ln_in_b"], p["w_in"], cos_t, sin_t, seq_len)
    shp = (bsz, seq_len, RET_WIDTH)
    ret = _retention(q.reshape(shp), k.reshape(shp), v.reshape(shp), sg.reshape(shp),
                     p["dl_f"], p["dl_b"]).reshape(n, RET_WIDTH)

    h2, idx8, rank8, gates, cnt = _mix_router(
        x2, ret, u, p["ln_in_g"], p["ln_in_b"], p["w_pool"], p["pool_scale"], p["w_o"],
        p["ln_mix_g"], p["ln_mix_b"], p["wr_hi"], p["wr_lo"], p["br_full"], seq_len)

    counts = cnt[:, 0]
    starts = jnp.cumsum(counts) - counts
    idx = idx8[:TOP_K]
    eq = idx[None, :, :] == jnp.arange(N_EXPERTS, dtype=jnp.int32)[:, None, None]
    slots = rank8[:TOP_K] + jnp.sum(jnp.where(eq, starts[:, None, None], 0), axis=0)

    xs = _dispatch(slots, h2)
    items = _expert_items(counts, n * TOP_K)
    ys = _experts(items, xs, p["w_gate"], p["b_gate"], p["w_up"], p["b_up"],
                  p["w_down"], p["b_down"])
    out = _combine(slots, h2, gates, p["ln_ffn_g"], p["ln_ffn_b"], ys)
    return out.reshape(bsz, seq_len, D_MODEL)


def kernel(x_prompt, x_sample, ln_in_g, ln_in_b, w_in, decay_logit_fwd, decay_logit_bwd, w_pool,
           pool_scale, w_o, ln_mix_g, ln_mix_b, w_router, b_router, w_gate, b_gate, w_up, b_up,
           w_down, b_down, ln_ffn_g, ln_ffn_b):
    row = lambda a: a.reshape(1, -1)
    wr = w_router[0].T
    wr_hi = _bf16(wr)
    wr_lo = _bf16(wr - wr_hi.astype(jnp.float32))
    p = {
        "ln_in_g": row(ln_in_g), "ln_in_b": row(ln_in_b),
        "w_in": _bf16(w_in[0]),
        "dl_f": decay_logit_fwd[0], "dl_b": decay_logit_bwd[0],
        "w_pool": _bf16(w_pool[0]), "pool_scale": row(pool_scale[0]),
        "w_o": _bf16(w_o[0]),
        "ln_mix_g": row(ln_mix_g[0]), "ln_mix_b": row(ln_mix_b[0]),
        "wr_hi": wr_hi, "wr_lo": wr_lo,
        "br_full": jnp.broadcast_to(b_router[0][:, None], (N_EXPERTS, TM_MIX)),
        "w_gate": _bf16(w_gate[0]), "b_gate": b_gate[0][:, None, :],
        "w_up": _bf16(w_up[0]), "b_up": b_up[0][:, None, :],
        "w_down": _bf16(w_down[0]), "b_down": b_down[0][:, None, :],
        "ln_ffn_g": row(ln_ffn_g[0]), "ln_ffn_b": row(ln_ffn_b[0]),
    }
    return (_trunk(x_prompt, p), _trunk(x_sample, p))
```

```python
import functools

import jax
import jax.numpy as jnp
import numpy as np
from jax import lax
from jax.experimental import pallas as pl
from jax.experimental.pallas import tpu as pltpu
from jax.experimental.pallas import tpu_sc as plsc

D_MODEL = 1024
RET_WIDTH = 512
POOL_WIDTH = 512
N_HEADS = 4
HEAD_DIM = 128
ROPE_BASE = 10000.0
POOL_WINDOWS = (2, 4, 8, 16)
POOL_GROUP = 128
IN_WIDTH = 4 * RET_WIDTH + POOL_WIDTH
N_EXPERTS = 32
TOP_K = 4
SWIGLU_ALPHA = 1.702
SWIGLU_LIMIT = 7.0
LN_EPS = 1e-5
DN_ALPHA = 2.0 ** 0.25

TM_PROJ = 512
RET_TILE = 1024
RET_CHUNK = 256
TM_MIX = 512
POOL_HALO = 16
TM_EXPERT = 512
TM_COMBINE = 512

VMEM_LIMIT = 56 * 1024 * 1024

_NT = (((1,), (1,)), ((), ()))
_TN = (((0,), (0,)), ((), ()))


def _layer_norm(x, g, b):
    mu = jnp.mean(x, axis=-1, keepdims=True)
    xc = x - mu
    var = jnp.mean(xc * xc, axis=-1, keepdims=True)
    return xc * lax.rsqrt(var + LN_EPS) * g + b


def _bf16(x):
    return x.astype(jnp.bfloat16)


def _dot(a, b):
    return jnp.dot(a, b, preferred_element_type=jnp.float32)


HALF = D_MODEL // 2


def _pack_rows(x):
    lo = lax.bitcast_convert_type(_bf16(x[:, :HALF]).astype(jnp.float32), jnp.uint32)
    hi = lax.bitcast_convert_type(_bf16(x[:, HALF:]).astype(jnp.float32), jnp.uint32)
    return lax.bitcast_convert_type((lo >> 16) | hi, jnp.int32)


def _unpack_rows(w):
    u = lax.bitcast_convert_type(w, jnp.uint32)
    lo = lax.bitcast_convert_type(u << 16, jnp.float32)
    hi = lax.bitcast_convert_type(u & jnp.uint32(0xFFFF0000), jnp.float32)
    return lo, hi


def _ln_proj_kernel(x_ref, g_ref, b_ref, w_ref, cos_ref, sin_ref,
                    q_ref, k_ref, v_ref, sg_ref, u_ref):
    h = _bf16(_layer_norm(x_ref[...], g_ref[...], b_ref[...]))
    cos = cos_ref[...]
    sin = sin_ref[...]

    def rotary(t):
        return t * cos + pltpu.roll(t, HEAD_DIM // 2, axis=1) * sin

    R = RET_WIDTH
    pq = _dot(h, w_ref[:, 0:R])
    for hd in range(N_HEADS):
        sl = slice(hd * HEAD_DIM, (hd + 1) * HEAD_DIM)
        q_ref[:, sl] = _bf16(rotary(pq[:, sl]))
    pk = _dot(h, w_ref[:, R:2 * R])
    for hd in range(N_HEADS):
        sl = slice(hd * HEAD_DIM, (hd + 1) * HEAD_DIM)
        k_ref[:, sl] = _bf16(rotary(pk[:, sl]) * (HEAD_DIM ** -0.5))
    v_ref[...] = _bf16(_dot(h, w_ref[:, 2 * R:3 * R]))
    pg = _dot(h, w_ref[:, 3 * R:4 * R])
    sg_ref[...] = _bf16(pg * jax.nn.sigmoid(pg))
    u_ref[...] = _bf16(_dot(h, w_ref[:, 4 * R:]))


def _ln_proj(x2, ln_g, ln_b, w_in_bf, cos_t, sin_t, seq_len):
    n = x2.shape[0]
    tm = TM_PROJ
    tiles_per_seq = seq_len // tm
    row = lambda i: (i, 0)
    const = lambda i: (0, 0)
    out_sd = jax.ShapeDtypeStruct((n, RET_WIDTH), jnp.bfloat16)
    return pl.pallas_call(
        _ln_proj_kernel,
        grid=(n // tm,),
        in_specs=[
            pl.BlockSpec((tm, D_MODEL), row),
            pl.BlockSpec((1, D_MODEL), const),
            pl.BlockSpec((1, D_MODEL), const),
            pl.BlockSpec((D_MODEL, IN_WIDTH), const),
            pl.BlockSpec((tm, HEAD_DIM), lambda i: (i % tiles_per_seq, 0)),
            pl.BlockSpec((tm, HEAD_DIM), lambda i: (i % tiles_per_seq, 0)),
        ],
        out_specs=[pl.BlockSpec((tm, RET_WIDTH), row)] * 5,
        out_shape=[out_sd] * 5,
        compiler_params=pltpu.CompilerParams(
            dimension_semantics=("arbitrary",), vmem_limit_bytes=VMEM_LIMIT),
        name="ln_proj",
    )(x2, ln_g, ln_b, w_in_bf, cos_t, sin_t)


def _log_sigmoid(x):
    return jnp.minimum(x, 0.0) - jnp.log(1.0 + jnp.exp(-jnp.abs(x)))


def _retention_kernel(dlf_ref, dlb_ref, q_ref, k_ref, v_ref, sg_ref, o_ref,
                      dmat, xi_f, xi_b, zeta_f, zeta_b, cdec, s_f, s_b, snap):
    b = pl.program_id(0)
    phase = pl.program_id(1)
    j = pl.program_id(2)
    n_tiles = pl.num_programs(2)
    C = RET_CHUNK
    sub = RET_TILE // C

    @pl.when((b == 0) & (phase == 0) & (j == 0))
    def _init_tables():
        ri = lax.broadcasted_iota(jnp.int32, (C, C), 0).astype(jnp.float32)
        ci = lax.broadcasted_iota(jnp.int32, (C, C), 1).astype(jnp.float32)
        rel = ri - ci
        pos = lax.broadcasted_iota(jnp.int32, (C, HEAD_DIM), 0).astype(jnp.float32)
        zero_cc = jnp.zeros((C, C), jnp.float32)
        zero_cd = jnp.zeros((C, HEAD_DIM), jnp.float32)
        zero_dd = jnp.zeros((HEAD_DIM, HEAD_DIM), jnp.float32)
        for hd in range(N_HEADS):
            dlf = dlf_ref[hd]
            dlb = dlb_ref[hd]
            dmat[hd] = jnp.where(rel >= 0, jnp.exp(_log_sigmoid(zero_cc + dlf) * rel),
                                 jnp.exp(-_log_sigmoid(zero_cc + dlb) * rel))
            lf = _log_sigmoid(zero_cd + dlf)
            lb = _log_sigmoid(zero_cd + dlb)
            xi_f[hd] = jnp.exp(lf * (pos + 1.0))
            xi_b[hd] = jnp.exp(lb * (C - pos))
            zeta_f[hd] = jnp.exp(lf * (C - 1.0 - pos))
            zeta_b[hd] = jnp.exp(lb * pos)
            cdec[hd, 0] = jnp.exp(_log_sigmoid(zero_dd + dlf) * float(C))
            cdec[hd, 1] = jnp.exp(_log_sigmoid(zero_dd + dlb) * float(C))

    @pl.when(j == 0)
    def _reset_state():
        @pl.when(phase == 0)
        def _():
            s_b[...] = jnp.zeros_like(s_b)

        @pl.when(phase == 1)
        def _():
            s_f[...] = jnp.zeros_like(s_f)

    @pl.when(phase == 0)
    def _backward_states():
        tile = n_tiles - 1 - j

        def body(cc, carry):
            c = sub - 1 - cc
            r0 = pl.multiple_of(c * C, C)
            chunk = tile * sub + c
            for hd in range(N_HEADS):
                sl = slice(hd * HEAD_DIM, (hd + 1) * HEAD_DIM)
                kc = k_ref[0, pl.ds(r0, C), sl]
                vc = v_ref[0, pl.ds(r0, C), sl].astype(jnp.float32)
                st = s_b[hd]
                snap[chunk * N_HEADS + hd] = _bf16(st)
                kv = lax.dot_general(kc, _bf16(vc * zeta_b[hd]), _TN,
                                     preferred_element_type=jnp.float32)
                s_b[hd] = st * cdec[hd, 1] + kv
            return carry

        lax.fori_loop(0, sub, body, 0)

    @pl.when(phase == 1)
    def _forward_outputs():
        def body(c, carry):
            r0 = pl.multiple_of(c * C, C)
            chunk = j * sub + c
            for hd in range(N_HEADS):
                sl = slice(hd * HEAD_DIM, (hd + 1) * HEAD_DIM)
                qc = q_ref[0, pl.ds(r0, C), sl]
                kc = k_ref[0, pl.ds(r0, C), sl]
                vc = v_ref[0, pl.ds(r0, C), sl]
                scores = lax.dot_general(qc, kc, _NT, preferred_element_type=jnp.float32)
                o = _dot(_bf16(scores * dmat[hd]), vc)
                st = s_f[hd]
                o = o + xi_f[hd] * _dot(qc, _bf16(st))
                o = o + xi_b[hd] * _dot(qc, snap[chunk * N_HEADS + hd])
                kv = lax.dot_general(kc, _bf16(vc.astype(jnp.float32) * zeta_f[hd]), _TN,
                                     preferred_element_type=jnp.float32)
                s_f[hd] = st * cdec[hd, 0] + kv
                mu = jnp.mean(o, axis=-1, keepdims=True)
                oc = o - mu
                on = oc * lax.rsqrt(jnp.mean(oc * oc, axis=-1, keepdims=True) + LN_EPS)
                sg = sg_ref[0, pl.ds(r0, C), sl].astype(jnp.float32)
                o_ref[0, pl.ds(r0, C), sl] = _bf16(sg * on)
            return carry

        lax.fori_loop(0, sub, body, 0)


def _retention(q, k, v, sg, dl_f, dl_b):
    bsz, seq_len, _ = q.shape
    n_tiles = seq_len // RET_TILE
    n_chunks = seq_len // RET_CHUNK
    C = RET_CHUNK

    def kv_map(b, p, j):
        return (b, jnp.where(p == 0, n_tiles - 1 - j, j), 0)

    def q_map(b, p, j):
        return (b, jnp.where(p == 0, 0, j), 0)

    blk = (1, RET_TILE, RET_WIDTH)
    smem = pl.BlockSpec(memory_space=pltpu.SMEM)
    return pl.pallas_call(
        _retention_kernel,
        grid=(bsz, 2, n_tiles),
        in_specs=[smem, smem,
                  pl.BlockSpec(blk, q_map), pl.BlockSpec(blk, kv_map),
                  pl.BlockSpec(blk, kv_map), pl.BlockSpec(blk, q_map)],
        out_specs=pl.BlockSpec(blk, q_map),
        out_shape=jax.ShapeDtypeStruct(q.shape, jnp.bfloat16),
        scratch_shapes=[
            pltpu.VMEM((N_HEADS, C, C), jnp.float32),
            pltpu.VMEM((N_HEADS, C, HEAD_DIM), jnp.float32),
            pltpu.VMEM((N_HEADS, C, HEAD_DIM), jnp.float32),
            pltpu.VMEM((N_HEADS, C, HEAD_DIM), jnp.float32),
            pltpu.VMEM((N_HEADS, C, HEAD_DIM), jnp.float32),
            pltpu.VMEM((N_HEADS, 2, HEAD_DIM, HEAD_DIM), jnp.float32),
            pltpu.VMEM((N_HEADS, HEAD_DIM, HEAD_DIM), jnp.float32),
            pltpu.VMEM((N_HEADS, HEAD_DIM, HEAD_DIM), jnp.float32),
            pltpu.VMEM((n_chunks * N_HEADS, HEAD_DIM, HEAD_DIM), jnp.bfloat16),
        ],
        compiler_params=pltpu.CompilerParams(
            dimension_semantics=("arbitrary", "arbitrary", "arbitrary"),
            vmem_limit_bytes=VMEM_LIMIT),
        name="retention",
    )(dl_f, dl_b, q, k, v, sg)


def _mix_router_kernel(seq_len, x_ref, ret_ref, u_ref, up_ref, un_ref,
                       lng_ref, lnb_ref, wpool_ref, pscale_ref, wo_ref,
                       mg_ref, mb_ref, wr_hi_ref, wr_lo_ref, br_ref,
                       h2_ref, h2p_ref, idx_ref, rank_ref, gate_ref, cnt_ref,
                       band, tri, base):
    i = pl.program_id(0)
    tm = TM_MIX
    halo = POOL_HALO
    tiles_per_seq = seq_len // tm
    ts = i % tiles_per_seq

    @pl.when(i == 0)
    def _init():
        r = lax.broadcasted_iota(jnp.int32, (tm, tm + 2 * halo), 0)
        c = lax.broadcasted_iota(jnp.int32, (tm, tm + 2 * halo), 1) - halo
        for gi, w in enumerate(POOL_WINDOWS):
            inside = (c >= r - w // 2) & (c <= r + (w - 1 - w // 2))
            band[gi] = jnp.where(inside, 1.0, 0.0).astype(jnp.bfloat16)
        tr = lax.broadcasted_iota(jnp.int32, (tm, tm), 0)
        tc = lax.broadcasted_iota(jnp.int32, (tm, tm), 1)
        tri[...] = jnp.where(tr < tc, 1.0, 0.0).astype(jnp.bfloat16)
        base[...] = jnp.zeros_like(base)

    prev = jnp.where(ts == 0, jnp.zeros_like(up_ref[...]), up_ref[...])
    nxt = jnp.where(ts == tiles_per_seq - 1, jnp.zeros_like(un_ref[...]), un_ref[...])
    u_cur = u_ref[...]
    u_ext = jnp.concatenate([prev, u_cur, nxt], axis=0)
    pos = ts * tm + lax.broadcasted_iota(jnp.int32, (tm, POOL_GROUP), 0)
    mixed = []
    for gi, w in enumerate(POOL_WINDOWS):
        sl = slice(gi * POOL_GROUP, (gi + 1) * POOL_GROUP)
        lo = jnp.maximum(pos - w // 2, 0)
        hi = jnp.minimum(pos + (w - 1 - w // 2), seq_len - 1)
        cnt = (hi - lo + 1).astype(jnp.float32)
        wsum = _dot(band[gi], u_ext[:, sl])
        pooled = wsum / cnt - u_cur[:, sl].astype(jnp.float32)
        mx = _dot(_bf16(pooled), wpool_ref[gi]) * pscale_ref[:, sl]
        mixed.append(_bf16(mx))
    pool = jnp.concatenate(mixed, axis=1)

    m = _dot(ret_ref[...], wo_ref[0:RET_WIDTH, :]) + _dot(pool, wo_ref[RET_WIDTH:, :])
    h = _layer_norm(x_ref[...], lng_ref[...], lnb_ref[...])
    h2 = _layer_norm(DN_ALPHA * h + m, mg_ref[...], mb_ref[...])
    h2_ref[...] = h2
    h2p_ref[...] = _pack_rows(h2)

    h_hi = _bf16(h2)
    h_lo = _bf16(h2 - h_hi.astype(jnp.float32))
    wr_hi = wr_hi_ref[...]
    logits = (lax.dot_general(wr_hi, h_hi, _NT, preferred_element_type=jnp.float32)
              + lax.dot_general(wr_hi, h_lo, _NT, preferred_element_type=jnp.float32)
              + lax.dot_general(wr_lo_ref[...], h_hi, _NT, preferred_element_type=jnp.float32)
              + br_ref[...])

    e_iota = lax.broadcasted_iota(jnp.int32, (N_EXPERTS, tm), 0)
    k_iota = lax.broadcasted_iota(jnp.int32, (8, tm), 0)
    vals8 = jnp.zeros((8, tm), jnp.float32)
    idx8 = jnp.zeros((8, tm), jnp.int32)
    sels = []
    work = logits
    for kk in range(TOP_K):
        mval = jnp.max(work, axis=0, keepdims=True)
        midx = jnp.min(jnp.where(work == mval, e_iota, N_EXPERTS), axis=0, keepdims=True)
        sel = e_iota == midx
        sels.append(sel)
        vals8 = jnp.where(k_iota == kk, mval, vals8)
        idx8 = jnp.where(k_iota == kk, midx, idx8)
        work = jnp.where(sel, -jnp.inf, work)

    ex = jnp.where(k_iota < TOP_K, jnp.exp(vals8 - vals8[0:1, :]), 0.0)
    gates8 = ex / jnp.sum(ex, axis=0, keepdims=True)

    onehot = jnp.zeros((N_EXPERTS, tm), jnp.float32)
    for sel in sels:
        onehot = onehot + jnp.where(sel, 1.0, 0.0)
    cum = _dot(_bf16(onehot), tri[...])
    before = base[...] + cum
    rank8 = jnp.zeros((8, tm), jnp.int32)
    for kk, sel in enumerate(sels):
        rk = jnp.sum(jnp.where(sel, before, 0.0), axis=0, keepdims=True).astype(jnp.int32)
        rank8 = jnp.where(k_iota == kk, rk, rank8)
    total = cum[:, tm - 1:tm] + onehot[:, tm - 1:tm]
    new_base = base[...] + jnp.broadcast_to(total, (N_EXPERTS, tm))
    base[...] = new_base

    idx_ref[...] = idx8
    rank_ref[...] = rank8
    gpad = jnp.concatenate([gates8, jnp.zeros((128 - 8, tm), jnp.float32)], axis=0)
    gate_ref[...] = gpad.T
    cnt_ref[...] = new_base[:, 0:128].astype(jnp.int32)


def _mix_router(x2, ret, u, ln_g, ln_b, w_pool_bf, pool_scale, w_o_bf, mix_g, mix_b,
                wr_hi, wr_lo, br_full, seq_len):
    n = x2.shape[0]
    tm = TM_MIX
    halo = POOL_HALO
    hb = tm // halo
    n_halo_blocks = n // halo
    row = lambda i: (i, 0)
    const = lambda i: (0, 0)
    col = lambda i: (0, i)
    return pl.pallas_call(
        functools.partial(_mix_router_kernel, seq_len),
        grid=(n // tm,),
        in_specs=[
            pl.BlockSpec((tm, D_MODEL), row),
            pl.BlockSpec((tm, RET_WIDTH), row),
            pl.BlockSpec((tm, POOL_WIDTH), row),
            pl.BlockSpec((halo, POOL_WIDTH), lambda i: (jnp.maximum(i * hb - 1, 0), 0)),
            pl.BlockSpec((halo, POOL_WIDTH),
                         lambda i: (jnp.minimum((i + 1) * hb, n_halo_blocks - 1), 0)),
            pl.BlockSpec((1, D_MODEL), const),
            pl.BlockSpec((1, D_MODEL), const),
            pl.BlockSpec((len(POOL_WINDOWS), POOL_GROUP, POOL_GROUP), lambda i: (0, 0, 0)),
            pl.BlockSpec((1, POOL_WIDTH), const),
            pl.BlockSpec((D_MODEL, D_MODEL), const),
            pl.BlockSpec((1, D_MODEL), const),
            pl.BlockSpec((1, D_MODEL), const),
            pl.BlockSpec((N_EXPERTS, D_MODEL), const),
            pl.BlockSpec((N_EXPERTS, D_MODEL), const),
            pl.BlockSpec((N_EXPERTS, tm), const),
        ],
        out_specs=[
            pl.BlockSpec((tm, D_MODEL), row),
            pl.BlockSpec((tm, HALF), row),
            pl.BlockSpec((8, tm), col),
            pl.BlockSpec((8, tm), col),
            pl.BlockSpec((tm, 128), row),
            pl.BlockSpec((N_EXPERTS, 128), const),
        ],
        out_shape=[
            jax.ShapeDtypeStruct((n, D_MODEL), jnp.float32),
            jax.ShapeDtypeStruct((n, HALF), jnp.int32),
            jax.ShapeDtypeStruct((8, n), jnp.int32),
            jax.ShapeDtypeStruct((8, n), jnp.int32),
            jax.ShapeDtypeStruct((n, 128), jnp.float32),
            jax.ShapeDtypeStruct((N_EXPERTS, 128), jnp.int32),
        ],
        scratch_shapes=[
            pltpu.VMEM((len(POOL_WINDOWS), tm, tm + 2 * halo), jnp.bfloat16),
            pltpu.VMEM((tm, tm), jnp.bfloat16),
            pltpu.VMEM((N_EXPERTS, tm), jnp.float32),
        ],
        compiler_params=pltpu.CompilerParams(
            dimension_semantics=("arbitrary",), vmem_limit_bytes=VMEM_LIMIT),
        name="mix_router",
    )(x2, ret, u, u, u, ln_g, ln_b, w_pool_bf, pool_scale, w_o_bf, mix_g, mix_b,
      wr_hi, wr_lo, br_full)


def _expert_kernel(it_tile, it_exp, it_lo, it_hi, xs_ref, wg_ref, bg_ref, wu_ref, bu_ref,
                   wd_ref, bd_ref, ys_ref):
    i = pl.program_id(0)
    lo = it_lo[i]
    hi = it_hi[i]

    @pl.when(hi > lo)
    def _():
        x_lo, x_hi = _unpack_rows(xs_ref[...])
        x_lo = _bf16(x_lo)
        x_hi = _bf16(x_hi)
        gt = _dot(x_lo, wg_ref[0, :HALF, :]) + _dot(x_hi, wg_ref[0, HALF:, :]) + bg_ref[0]
        up = _dot(x_lo, wu_ref[0, :HALF, :]) + _dot(x_hi, wu_ref[0, HALF:, :]) + bu_ref[0]
        gt = jnp.minimum(gt, SWIGLU_LIMIT)
        up = jnp.clip(up, -SWIGLU_LIMIT, SWIGLU_LIMIT)
        hid = (up + 1.0) * (gt * jax.nn.sigmoid(SWIGLU_ALPHA * gt))
        y = _pack_rows(_dot(_bf16(hid), wd_ref[0]) + bd_ref[0])
        r = lax.broadcasted_iota(jnp.int32, y.shape, 0)
        mine = (r >= lo) & (r < hi)

        @pl.when(lo == 0)
        def _():
            ys_ref[...] = jnp.where(mine, y, 0)

        @pl.when(lo > 0)
        def _():
            ys_ref[...] = jnp.where(mine, y, ys_ref[...])


def _experts(items, xs, w_gate_bf, b_gate, w_up_bf, b_up, w_down_bf, b_down):
    it_tile, it_exp, it_lo, it_hi = items
    p = xs.shape[0]
    tm = TM_EXPERT
    n_items = it_tile.shape[0]
    tile_map = lambda i, t, e, lo, hi: (t[i], 0)
    w_map = lambda i, t, e, lo, hi: (e[i], 0, 0)
    w_spec = pl.BlockSpec((1, D_MODEL, D_MODEL), w_map)
    b_spec = pl.BlockSpec((1, 1, D_MODEL), w_map)
    grid_spec = pltpu.PrefetchScalarGridSpec(
        num_scalar_prefetch=4,
        grid=(n_items,),
        in_specs=[pl.BlockSpec((tm, HALF), tile_map),
                  w_spec, b_spec, w_spec, b_spec, w_spec, b_spec],
        out_specs=pl.BlockSpec((tm, HALF), tile_map),
    )
    return pl.pallas_call(
        _expert_kernel,
        grid_spec=grid_spec,
        out_shape=jax.ShapeDtypeStruct((p, HALF), jnp.int32),
        compiler_params=pltpu.CompilerParams(
            dimension_semantics=("arbitrary",), vmem_limit_bytes=VMEM_LIMIT),
        name="experts",
    )(it_tile, it_exp, it_lo, it_hi, xs, w_gate_bf, b_gate, w_up_bf, b_up, w_down_bf, b_down)


def _expert_items(counts, n_rows):
    tm = TM_EXPERT
    n_tiles = n_rows // tm
    n_items = n_tiles + N_EXPERTS - 1
    ends = jnp.cumsum(counts)
    starts = ends - counts
    first_tile = starts // tm
    last_tile = jnp.maximum(ends - 1, 0) // tm
    per_exp = jnp.where(counts > 0, last_tile - first_tile + 1, 0)
    item_end = jnp.cumsum(per_exp)
    item_start = item_end - per_exp
    total = item_end[-1]
    i = jnp.arange(n_items, dtype=jnp.int32)
    ic = jnp.minimum(i, total - 1)
    e = jnp.sum((ic[:, None] >= item_end[None, :]).astype(jnp.int32), axis=1)
    e = jnp.minimum(e, N_EXPERTS - 1)
    onehot = (e[:, None] == jnp.arange(N_EXPERTS, dtype=jnp.int32)[None, :]).astype(jnp.int32)
    pick = lambda a: jnp.sum(onehot * a[None, :], axis=1)
    tile = pick(first_tile) + ic - pick(item_start)
    lo = jnp.maximum(pick(starts), tile * tm) - tile * tm
    hi = jnp.minimum(pick(ends), (tile + 1) * tm) - tile * tm
    valid = i < total
    lo = jnp.where(valid, lo, 0)
    hi = jnp.where(valid, hi, 0)
    return (tile.astype(jnp.int32), e.astype(jnp.int32), lo.astype(jnp.int32),
            hi.astype(jnp.int32))


def _combine_kernel(h2_ref, gate_ref, g_ref, b_ref, yk_ref, o_ref):
    gates = gate_ref[...]
    h2 = h2_ref[...]
    f_lo = jnp.zeros((h2.shape[0], HALF), jnp.float32)
    f_hi = jnp.zeros((h2.shape[0], HALF), jnp.float32)
    for kk in range(TOP_K):
        y_lo, y_hi = _unpack_rows(yk_ref[kk])
        gk = gates[:, kk:kk + 1]
        f_lo = f_lo + gk * y_lo
        f_hi = f_hi + gk * y_hi
    y = DN_ALPHA * h2 + jnp.concatenate([f_lo, f_hi], axis=1)
    o_ref[...] = _layer_norm(y, g_ref[...], b_ref[...])


def _combine(h2, gates, ffn_g, ffn_b, yk):
    n = h2.shape[0]
    tm = TM_COMBINE
    row = lambda i: (i, 0)
    const = lambda i: (0, 0)
    return pl.pallas_call(
        _combine_kernel,
        grid=(n // tm,),
        in_specs=[
            pl.BlockSpec((tm, D_MODEL), row),
            pl.BlockSpec((tm, 128), row),
            pl.BlockSpec((1, D_MODEL), const),
            pl.BlockSpec((1, D_MODEL), const),
            pl.BlockSpec((TOP_K, tm, HALF), lambda i: (0, i, 0)),
        ],
        out_specs=pl.BlockSpec((tm, D_MODEL), row),
        out_shape=jax.ShapeDtypeStruct((n, D_MODEL), jnp.float32),
        compiler_params=pltpu.CompilerParams(
            dimension_semantics=("arbitrary",), vmem_limit_bytes=VMEM_LIMIT),
        name="combine",
    )(h2, gates, ffn_g, ffn_b, yk)


SC_CORES = 2
SC_SUBCORES = 16
SC_CHUNK = 64


def _chunked_slots(slots, ch):
    n = slots.shape[1]
    return slots.reshape(TOP_K, n // ch, ch).transpose(1, 0, 2)


def _sc_dispatch(slots, h2):
    n, w = h2.shape
    ch = SC_CHUNK
    n_chunks = n // (SC_CORES * SC_SUBCORES * ch)
    mesh = plsc.VectorSubcoreMesh(core_axis_name="c", subcore_axis_name="s")

    @functools.partial(
        pl.kernel, mesh=mesh,
        out_type=jax.ShapeDtypeStruct((n * TOP_K, w), h2.dtype),
        scratch_types=[pltpu.VMEM((TOP_K, ch), jnp.int32), pltpu.VMEM((ch, w), h2.dtype)],
        name="sc_dispatch")
    def k(idx_hbm, h2_hbm, xs_hbm, idx_v, rows_v):
        wid = lax.axis_index("s") * SC_CORES + lax.axis_index("c")

        @pl.loop(0, n_chunks)
        def _(c):
            chunk = wid * n_chunks + c
            pltpu.sync_copy(idx_hbm.at[chunk], idx_v)
            pltpu.sync_copy(h2_hbm.at[pl.ds(chunk * ch, ch)], rows_v)
            for kk in range(TOP_K):
                pltpu.sync_copy(rows_v, xs_hbm.at[idx_v.at[kk]])

    return k(_chunked_slots(slots, ch), h2)


def _sc_gather(slots, ys):
    n = slots.shape[1]
    w = ys.shape[1]
    ch = SC_CHUNK
    n_chunks = n // (SC_CORES * SC_SUBCORES * ch)
    mesh = plsc.VectorSubcoreMesh(core_axis_name="c", subcore_axis_name="s")

    @functools.partial(
        pl.kernel, mesh=mesh,
        out_type=jax.ShapeDtypeStruct((TOP_K, n, w), ys.dtype),
        scratch_types=[pltpu.VMEM((TOP_K, ch), jnp.int32), pltpu.VMEM((ch, w), ys.dtype)],
        name="sc_gather")
    def k(idx_hbm, ys_hbm, out_hbm, idx_v, rows_v):
        wid = lax.axis_index("s") * SC_CORES + lax.axis_index("c")

        @pl.loop(0, n_chunks)
        def _(c):
            chunk = wid * n_chunks + c
            pltpu.sync_copy(idx_hbm.at[chunk], idx_v)
            for kk in range(TOP_K):
                pltpu.sync_copy(ys_hbm.at[idx_v.at[kk]], rows_v)
                pltpu.sync_copy(rows_v, out_hbm.at[kk, pl.ds(chunk * ch, ch)])

    return k(_chunked_slots(slots, ch), ys)


def _rotary_tables(seq_len):
    d = HEAD_DIM
    inv_freq = ROPE_BASE ** (-jnp.arange(0, d, 2, dtype=jnp.float32) / d)
    ang = jnp.arange(seq_len, dtype=jnp.float32)[:, None] * inv_freq[None, :]
    cos = jnp.cos(ang)
    sin = jnp.sin(ang)
    return jnp.concatenate([cos, cos], axis=1), jnp.concatenate([-sin, sin], axis=1)


def _trunk(x, p):
    bsz, seq_len, _ = x.shape
    n = bsz * seq_len
    x2 = x.reshape(n, D_MODEL)
    cos_t, sin_t = _rotary_tables(seq_len)

    q, k, v, sg, u = _ln_proj(x2, p["ln_in_g"], p["ln_in_b"], p["w_in"], cos_t, sin_t, seq_len)
    shp = (bsz, seq_len, RET_WIDTH)
    ret = _retention(q.reshape(shp), k.reshape(shp), v.reshape(shp), sg.reshape(shp),
                     p["dl_f"], p["dl_b"]).reshape(n, RET_WIDTH)

    h2, h2p, idx8, rank8, gates, cnt = _mix_router(
        x2, ret, u, p["ln_in_g"], p["ln_in_b"], p["w_pool"], p["pool_scale"], p["w_o"],
        p["ln_mix_g"], p["ln_mix_b"], p["wr_hi"], p["wr_lo"], p["br_full"], seq_len)

    counts = cnt[:, 0]
    starts = jnp.cumsum(counts) - counts
    idx = idx8[:TOP_K]
    eq = idx[None, :, :] == jnp.arange(N_EXPERTS, dtype=jnp.int32)[:, None, None]
    slots = rank8[:TOP_K] + jnp.sum(jnp.where(eq, starts[:, None, None], 0), axis=0)

    xs = _sc_dispatch(slots, h2p)
    items = _expert_items(counts, n * TOP_K)
    ys = _experts(items, xs, p["w_gate"], p["b_gate"], p["w_up"], p["b_up"],
                  p["w_down"], p["b_down"])
    yk = _sc_gather(slots, ys)
    out = _combine(h2, gates, p["ln_ffn_g"], p["ln_ffn_b"], yk)
    return out.reshape(bsz, seq_len, D_MODEL)


def kernel(x_prompt, x_sample, ln_in_g, ln_in_b, w_in, decay_logit_fwd, decay_logit_bwd, w_pool,
           pool_scale, w_o, ln_mix_g, ln_mix_b, w_router, b_router, w_gate, b_gate, w_up, b_up,
           w_down, b_down, ln_ffn_g, ln_ffn_b):
    row = lambda a: a.reshape(1, -1)
    wr = w_router[0].T
    wr_hi = _bf16(wr)
    wr_lo = _bf16(wr - wr_hi.astype(jnp.float32))
    p = {
        "ln_in_g": row(ln_in_g), "ln_in_b": row(ln_in_b),
        "w_in": _bf16(w_in[0]),
        "dl_f": decay_logit_fwd[0], "dl_b": decay_logit_bwd[0],
        "w_pool": _bf16(w_pool[0]), "pool_scale": row(pool_scale[0]),
        "w_o": _bf16(w_o[0]),
        "ln_mix_g": row(ln_mix_g[0]), "ln_mix_b": row(ln_mix_b[0]),
        "wr_hi": wr_hi, "wr_lo": wr_lo,
        "br_full": jnp.broadcast_to(b_router[0][:, None], (N_EXPERTS, TM_MIX)),
        "w_gate": _bf16(w_gate[0]), "b_gate": b_gate[0][:, None, :],
        "w_up": _bf16(w_up[0]), "b_up": b_up[0][:, None, :],
        "w_down": _bf16(w_down[0]), "b_down": b_down[0][:, None, :],
        "ln_ffn_g": row(ln_ffn_g[0]), "ln_ffn_b": row(ln_ffn_b[0]),
    }
    return (_trunk(x_prompt, p), _trunk(x_sample, p))
```

```python
import functools

import jax
import jax.numpy as jnp
import numpy as np
from jax import lax
from jax.experimental import pallas as pl
from jax.experimental.pallas import tpu as pltpu
from jax.experimental.pallas import tpu_sc as plsc

D_MODEL = 1024
RET_WIDTH = 512
POOL_WIDTH = 512
N_HEADS = 4
HEAD_DIM = 128
ROPE_BASE = 10000.0
POOL_WINDOWS = (2, 4, 8, 16)
POOL_GROUP = 128
IN_WIDTH = 4 * RET_WIDTH + POOL_WIDTH
N_EXPERTS = 32
TOP_K = 4
SWIGLU_ALPHA = 1.702
SWIGLU_LIMIT = 7.0
LN_EPS = 1e-5
DN_ALPHA = 2.0 ** 0.25

TM_PROJ = 512
RET_TILE = 1024
RET_CHUNK = 256
TM_MIX = 512
MIX_SUB = 512
POOL_HALO = 16
TM_EXPERT = 512
EXPERT_SPLIT = 1
TM_COMBINE = 512

VMEM_LIMIT = 56 * 1024 * 1024

_NT = (((1,), (1,)), ((), ()))
_TN = (((0,), (0,)), ((), ()))


def _layer_norm(x, g, b):
    mu = jnp.mean(x, axis=-1, keepdims=True)
    xc = x - mu
    var = jnp.mean(xc * xc, axis=-1, keepdims=True)
    return xc * lax.rsqrt(var + LN_EPS) * g + b


def _bf16(x):
    return x.astype(jnp.bfloat16)


def _dot(a, b):
    return jnp.dot(a, b, preferred_element_type=jnp.float32)


HALF = D_MODEL // 2


def _pack_rows(x):
    lo = lax.bitcast_convert_type(_bf16(x[:, :HALF]).astype(jnp.float32), jnp.uint32)
    hi = lax.bitcast_convert_type(_bf16(x[:, HALF:]).astype(jnp.float32), jnp.uint32)
    return lax.bitcast_convert_type((lo >> 16) | hi, jnp.int32)


def _unpack_rows(w):
    u = lax.bitcast_convert_type(w, jnp.uint32)
    lo = lax.bitcast_convert_type(u << 16, jnp.float32)
    hi = lax.bitcast_convert_type(u & jnp.uint32(0xFFFF0000), jnp.float32)
    return lo, hi


def _ln_proj_kernel(x_ref, g_ref, b_ref, w_ref, cos_ref, sin_ref,
                    q_ref, k_ref, v_ref, sg_ref, u_ref):
    h = _bf16(_layer_norm(x_ref[...], g_ref[...], b_ref[...]))
    cos = cos_ref[...]
    sin = sin_ref[...]

    def rotary(t):
        return t * cos + pltpu.roll(t, HEAD_DIM // 2, axis=1) * sin

    R = RET_WIDTH
    pq = _dot(h, w_ref[:, 0:R])
    for hd in range(N_HEADS):
        sl = slice(hd * HEAD_DIM, (hd + 1) * HEAD_DIM)
        q_ref[:, sl] = _bf16(rotary(pq[:, sl]))
    pk = _dot(h, w_ref[:, R:2 * R])
    for hd in range(N_HEADS):
        sl = slice(hd * HEAD_DIM, (hd + 1) * HEAD_DIM)
        k_ref[:, sl] = _bf16(rotary(pk[:, sl]) * (HEAD_DIM ** -0.5))
    v_ref[...] = _bf16(_dot(h, w_ref[:, 2 * R:3 * R]))
    pg = _dot(h, w_ref[:, 3 * R:4 * R])
    sg_ref[...] = _bf16(pg * jax.nn.sigmoid(pg))
    u_ref[...] = _bf16(_dot(h, w_ref[:, 4 * R:]))


def _ln_proj(x2, ln_g, ln_b, w_in_bf, cos_t, sin_t, seq_len):
    n = x2.shape[0]
    tm = TM_PROJ
    tiles_per_seq = seq_len // tm
    row = lambda i: (i, 0)
    const = lambda i: (0, 0)
    out_sd = jax.ShapeDtypeStruct((n, RET_WIDTH), jnp.bfloat16)
    return pl.pallas_call(
        _ln_proj_kernel,
        grid=(n // tm,),
        in_specs=[
            pl.BlockSpec((tm, D_MODEL), row),
            pl.BlockSpec((1, D_MODEL), const),
            pl.BlockSpec((1, D_MODEL), const),
            pl.BlockSpec((D_MODEL, IN_WIDTH), const),
            pl.BlockSpec((tm, HEAD_DIM), lambda i: (i % tiles_per_seq, 0)),
            pl.BlockSpec((tm, HEAD_DIM), lambda i: (i % tiles_per_seq, 0)),
        ],
        out_specs=[pl.BlockSpec((tm, RET_WIDTH), row)] * 5,
        out_shape=[out_sd] * 5,
        compiler_params=pltpu.CompilerParams(
            dimension_semantics=("arbitrary",), vmem_limit_bytes=VMEM_LIMIT),
        name="ln_proj",
    )(x2, ln_g, ln_b, w_in_bf, cos_t, sin_t)


def _log_sigmoid(x):
    return jnp.minimum(x, 0.0) - jnp.log(1.0 + jnp.exp(-jnp.abs(x)))


def _retention_kernel(dlf_ref, dlb_ref, q_ref, k_ref, v_ref, sg_ref, o_ref,
                      dmat, xi_f, xi_b, zeta_f, zeta_b, cdec, s_f, s_b, snap):
    b = pl.program_id(0)
    phase = pl.program_id(1)
    j = pl.program_id(2)
    n_tiles = pl.num_programs(2)
    C = RET_CHUNK
    sub = RET_TILE // C

    @pl.when((b == 0) & (phase == 0) & (j == 0))
    def _init_tables():
        ri = lax.broadcasted_iota(jnp.int32, (C, C), 0).astype(jnp.float32)
        ci = lax.broadcasted_iota(jnp.int32, (C, C), 1).astype(jnp.float32)
        rel = ri - ci
        pos = lax.broadcasted_iota(jnp.int32, (C, HEAD_DIM), 0).astype(jnp.float32)
        zero_cc = jnp.zeros((C, C), jnp.float32)
        zero_cd = jnp.zeros((C, HEAD_DIM), jnp.float32)
        zero_dd = jnp.zeros((HEAD_DIM, HEAD_DIM), jnp.float32)
        for hd in range(N_HEADS):
            dlf = dlf_ref[hd]
            dlb = dlb_ref[hd]
            dmat[hd] = jnp.where(rel >= 0, jnp.exp(_log_sigmoid(zero_cc + dlf) * rel),
                                 jnp.exp(-_log_sigmoid(zero_cc + dlb) * rel))
            lf = _log_sigmoid(zero_cd + dlf)
            lb = _log_sigmoid(zero_cd + dlb)
            xi_f[hd] = jnp.exp(lf * (pos + 1.0))
            xi_b[hd] = jnp.exp(lb * (C - pos))
            zeta_f[hd] = jnp.exp(lf * (C - 1.0 - pos))
            zeta_b[hd] = jnp.exp(lb * pos)
            cdec[hd, 0] = jnp.exp(_log_sigmoid(zero_dd + dlf) * float(C))
            cdec[hd, 1] = jnp.exp(_log_sigmoid(zero_dd + dlb) * float(C))

    @pl.when(j == 0)
    def _reset_state():
        @pl.when(phase == 0)
        def _():
            s_b[...] = jnp.zeros_like(s_b)

        @pl.when(phase == 1)
        def _():
            s_f[...] = jnp.zeros_like(s_f)

    @pl.when(phase == 0)
    def _backward_states():
        tile = n_tiles - 1 - j

        def body(cc, carry):
            c = sub - 1 - cc
            r0 = pl.multiple_of(c * C, C)
            chunk = tile * sub + c
            for hd in range(N_HEADS):
                sl = slice(hd * HEAD_DIM, (hd + 1) * HEAD_DIM)
                kc = k_ref[0, pl.ds(r0, C), sl]
                vc = v_ref[0, pl.ds(r0, C), sl].astype(jnp.float32)
                st = s_b[hd]
                snap[chunk * N_HEADS + hd] = _bf16(st)
                kv = lax.dot_general(kc, _bf16(vc * zeta_b[hd]), _TN,
                                     preferred_element_type=jnp.float32)
                s_b[hd] = st * cdec[hd, 1] + kv
            return carry

        lax.fori_loop(0, sub, body, 0, unroll=True)

    @pl.when(phase == 1)
    def _forward_outputs():
        def body(c, carry):
            r0 = pl.multiple_of(c * C, C)
            chunk = j * sub + c
            for hd in range(N_HEADS):
                sl = slice(hd * HEAD_DIM, (hd + 1) * HEAD_DIM)
                qc = q_ref[0, pl.ds(r0, C), sl]
                kc = k_ref[0, pl.ds(r0, C), sl]
                vc = v_ref[0, pl.ds(r0, C), sl]
                scores = lax.dot_general(qc, kc, _NT, preferred_element_type=jnp.float32)
                o = _dot(_bf16(scores * dmat[hd]), vc)
                st = s_f[hd]
                both = jnp.concatenate([_bf16(st), snap[chunk * N_HEADS + hd]], axis=1)
                cross = _dot(qc, both)
                o = o + xi_f[hd] * cross[:, :HEAD_DIM] + xi_b[hd] * cross[:, HEAD_DIM:]
                kv = lax.dot_general(kc, _bf16(vc.astype(jnp.float32) * zeta_f[hd]), _TN,
                                     preferred_element_type=jnp.float32)
                s_f[hd] = st * cdec[hd, 0] + kv
                mu = jnp.mean(o, axis=-1, keepdims=True)
                oc = o - mu
                on = oc * lax.rsqrt(jnp.mean(oc * oc, axis=-1, keepdims=True) + LN_EPS)
                sg = sg_ref[0, pl.ds(r0, C), sl].astype(jnp.float32)
                o_ref[0, pl.ds(r0, C), sl] = _bf16(sg * on)
            return carry

        lax.fori_loop(0, sub, body, 0, unroll=True)


def _retention(q, k, v, sg, dl_f, dl_b):
    bsz, seq_len, _ = q.shape
    n_tiles = seq_len // RET_TILE
    n_chunks = seq_len // RET_CHUNK
    C = RET_CHUNK

    def kv_map(b, p, j):
        return (b, jnp.where(p == 0, n_tiles - 1 - j, j), 0)

    def q_map(b, p, j):
        return (b, jnp.where(p == 0, 0, j), 0)

    blk = (1, RET_TILE, RET_WIDTH)
    smem = pl.BlockSpec(memory_space=pltpu.SMEM)
    return pl.pallas_call(
        _retention_kernel,
        grid=(bsz, 2, n_tiles),
        in_specs=[smem, smem,
                  pl.BlockSpec(blk, q_map), pl.BlockSpec(blk, kv_map),
                  pl.BlockSpec(blk, kv_map), pl.BlockSpec(blk, q_map)],
        out_specs=pl.BlockSpec(blk, q_map),
        out_shape=jax.ShapeDtypeStruct(q.shape, jnp.bfloat16),
        scratch_shapes=[
            pltpu.VMEM((N_HEADS, C, C), jnp.float32),
            pltpu.VMEM((N_HEADS, C, HEAD_DIM), jnp.float32),
            pltpu.VMEM((N_HEADS, C, HEAD_DIM), jnp.float32),
            pltpu.VMEM((N_HEADS, C, HEAD_DIM), jnp.float32),
            pltpu.VMEM((N_HEADS, C, HEAD_DIM), jnp.float32),
            pltpu.VMEM((N_HEADS, 2, HEAD_DIM, HEAD_DIM), jnp.float32),
            pltpu.VMEM((N_HEADS, HEAD_DIM, HEAD_DIM), jnp.float32),
            pltpu.VMEM((N_HEADS, HEAD_DIM, HEAD_DIM), jnp.float32),
            pltpu.VMEM((n_chunks * N_HEADS, HEAD_DIM, HEAD_DIM), jnp.bfloat16),
        ],
        compiler_params=pltpu.CompilerParams(
            dimension_semantics=("arbitrary", "arbitrary", "arbitrary"),
            vmem_limit_bytes=VMEM_LIMIT),
        name="retention",
    )(dl_f, dl_b, q, k, v, sg)


def _mix_router_kernel(seq_len, x_ref, ret_ref, u_ref, up_ref, un_ref, inv_ref,
                       lng_ref, lnb_ref, wpool_ref, pscale_ref, wo_ref,
                       mg_ref, mb_ref, wr_hi_ref, wr_cat_ref, br_ref,
                       h2_ref, h2p_ref, idx_ref, rank_ref, gate_ref, cnt_ref,
                       tri, base):
    i = pl.program_id(0)
    tm = TM_MIX
    sb = MIX_SUB
    n_sub = tm // sb
    halo = POOL_HALO
    tiles_per_seq = seq_len // tm
    ts = i % tiles_per_seq

    @pl.when(i == 0)
    def _init():
        tr = lax.broadcasted_iota(jnp.int32, (sb, sb), 0)
        tc = lax.broadcasted_iota(jnp.int32, (sb, sb), 1)
        tri[...] = jnp.where(tr < tc, 1.0, 0.0).astype(jnp.bfloat16)
        base[...] = jnp.zeros_like(base)

    prev = jnp.where(ts == 0, jnp.zeros_like(up_ref[...]), up_ref[...])
    nxt = jnp.where(ts == tiles_per_seq - 1, jnp.zeros_like(un_ref[...]), un_ref[...])
    e_iota = lax.broadcasted_iota(jnp.int32, (N_EXPERTS, sb), 0)
    k_iota = lax.broadcasted_iota(jnp.int32, (8, sb), 0)
    wr_hi = wr_hi_ref[...]
    wr_cat = wr_cat_ref[...]
    running = base[...]

    for s in range(n_sub):
        r0 = s * sb
        rows = slice(r0, r0 + sb)
        u_cur = u_ref[rows, :]
        above = prev if s == 0 else u_ref[r0 - halo:r0, :]
        below = nxt if s == n_sub - 1 else u_ref[r0 + sb:r0 + sb + halo, :]
        u_ext = jnp.concatenate([above, u_cur, below], axis=0).astype(jnp.float32)
        n_ext = sb + 2 * halo

        sums = []
        for gi, w in enumerate(POOL_WINDOWS):
            a = u_ext[:, gi * POOL_GROUP:(gi + 1) * POOL_GROUP]
            span = 1
            while span < w // 2:
                a = a + pltpu.roll(a, n_ext - span, axis=0)
                span *= 2
            sums.append(a + pltpu.roll(a, w // 2, axis=0))
        wsum = jnp.concatenate(sums, axis=1)[halo:halo + sb, :]
        pooled = wsum * inv_ref[0, rows, :] - u_cur.astype(jnp.float32)
        pool = _bf16(_dot(_bf16(pooled), wpool_ref[...]) * pscale_ref[...])

        m = _dot(ret_ref[rows, :], wo_ref[0:RET_WIDTH, :]) + _dot(pool, wo_ref[RET_WIDTH:, :])
        h = _layer_norm(x_ref[rows, :], lng_ref[...], lnb_ref[...])
        h2 = _layer_norm(DN_ALPHA * h + m, mg_ref[...], mb_ref[...])
        h2_ref[rows, :] = h2
        h2p_ref[rows, :] = _pack_rows(h2)

        h_hi = _bf16(h2)
        h_lo = _bf16(h2 - h_hi.astype(jnp.float32))
        part = _dot(h_hi, wr_cat)
        lrow = part + pltpu.roll(part, 128 - N_EXPERTS, axis=1) + _dot(h_lo, wr_hi)
        logits = lrow.T[0:N_EXPERTS, :] + br_ref[...]

        vals8 = jnp.zeros((8, sb), jnp.float32)
        idx8 = jnp.zeros((8, sb), jnp.int32)
        sels = []
        work = logits
        for kk in range(TOP_K):
            mval = jnp.max(work, axis=0, keepdims=True)
            midx = jnp.min(jnp.where(work == mval, e_iota, N_EXPERTS), axis=0, keepdims=True)
            sel = e_iota == midx
            sels.append(sel)
            vals8 = jnp.where(k_iota == kk, mval, vals8)
            idx8 = jnp.where(k_iota == kk, midx, idx8)
            work = jnp.where(sel, -jnp.inf, work)

        ex = jnp.where(k_iota < TOP_K, jnp.exp(vals8 - vals8[0:1, :]), 0.0)
        gates8 = ex / jnp.sum(ex, axis=0, keepdims=True)

        onehot = jnp.zeros((N_EXPERTS, sb), jnp.float32)
        for sel in sels:
            onehot = onehot + jnp.where(sel, 1.0, 0.0)
        cum = _dot(_bf16(onehot), tri[...])
        before = running + cum
        rank8 = jnp.zeros((8, sb), jnp.int32)
        for kk, sel in enumerate(sels):
            rk = jnp.sum(jnp.where(sel, before, 0.0), axis=0, keepdims=True).astype(jnp.int32)
            rank8 = jnp.where(k_iota == kk, rk, rank8)
        total = cum[:, sb - 1:sb] + onehot[:, sb - 1:sb]
        running = running + jnp.broadcast_to(total, (N_EXPERTS, sb))

        idx_ref[:, rows] = idx8
        rank_ref[:, rows] = rank8
        gpad = jnp.concatenate([gates8, jnp.zeros((128 - 8, sb), jnp.float32)], axis=0)
        gate_ref[rows, :] = gpad.T

    base[...] = running
    cnt_ref[...] = running[:, 0:128].astype(jnp.int32)


def _pool_inv_counts(seq_len):
    tm = TM_MIX
    assert seq_len // tm >= 3
    cols = []
    for w in POOL_WINDOWS:
        pos = np.arange(seq_len)
        lo = np.clip(pos - w // 2, 0, seq_len - 1)
        hi = np.clip(pos + (w - 1 - w // 2), 0, seq_len - 1)
        inv = (1.0 / (hi - lo + 1)).astype(np.float32)
        kinds = np.stack([inv[:tm], inv[tm:2 * tm], inv[seq_len - tm:]])
        cols.append(np.repeat(kinds[:, :, None], POOL_GROUP, axis=2))
    return jnp.asarray(np.concatenate(cols, axis=2))


def _mix_router(x2, ret, u, ln_g, ln_b, w_pool_bf, pool_scale, w_o_bf, mix_g, mix_b,
                wr_hi, wr_cat, br_full, seq_len):
    n = x2.shape[0]
    tm = TM_MIX
    sb = MIX_SUB
    halo = POOL_HALO
    hb = tm // halo
    n_halo_blocks = n // halo
    tiles_per_seq = seq_len // tm
    row = lambda i: (i, 0)
    const = lambda i: (0, 0)
    col = lambda i: (0, i)

    def inv_map(i):
        ts = i % tiles_per_seq
        return (jnp.where(ts == 0, 0, jnp.where(ts == tiles_per_seq - 1, 2, 1)), 0, 0)

    return pl.pallas_call(
        functools.partial(_mix_router_kernel, seq_len),
        grid=(n // tm,),
        in_specs=[
            pl.BlockSpec((tm, D_MODEL), row),
            pl.BlockSpec((tm, RET_WIDTH), row),
            pl.BlockSpec((tm, POOL_WIDTH), row),
            pl.BlockSpec((halo, POOL_WIDTH), lambda i: (jnp.maximum(i * hb - 1, 0), 0)),
            pl.BlockSpec((halo, POOL_WIDTH),
                         lambda i: (jnp.minimum((i + 1) * hb, n_halo_blocks - 1), 0)),
            pl.BlockSpec((1, tm, POOL_WIDTH), inv_map),
            pl.BlockSpec((1, D_MODEL), const),
            pl.BlockSpec((1, D_MODEL), const),
            pl.BlockSpec((POOL_WIDTH, POOL_WIDTH), const),
            pl.BlockSpec((1, POOL_WIDTH), const),
            pl.BlockSpec((D_MODEL, D_MODEL), const),
            pl.BlockSpec((1, D_MODEL), const),
            pl.BlockSpec((1, D_MODEL), const),
            pl.BlockSpec((D_MODEL, 128), const),
            pl.BlockSpec((D_MODEL, 128), const),
            pl.BlockSpec((N_EXPERTS, sb), const),
        ],
        out_specs=[
            pl.BlockSpec((tm, D_MODEL), row),
            pl.BlockSpec((tm, HALF), row),
            pl.BlockSpec((8, tm), col),
            pl.BlockSpec((8, tm), col),
            pl.BlockSpec((tm, 128), row),
            pl.BlockSpec((N_EXPERTS, 128), const),
        ],
        out_shape=[
            jax.ShapeDtypeStruct((n, D_MODEL), jnp.float32),
            jax.ShapeDtypeStruct((n, HALF), jnp.int32),
            jax.ShapeDtypeStruct((8, n), jnp.int32),
            jax.ShapeDtypeStruct((8, n), jnp.int32),
            jax.ShapeDtypeStruct((n, 128), jnp.float32),
            jax.ShapeDtypeStruct((N_EXPERTS, 128), jnp.int32),
        ],
        scratch_shapes=[
            pltpu.VMEM((sb, sb), jnp.bfloat16),
            pltpu.VMEM((N_EXPERTS, sb), jnp.float32),
        ],
        compiler_params=pltpu.CompilerParams(
            dimension_semantics=("arbitrary",), vmem_limit_bytes=VMEM_LIMIT),
        name="mix_router",
    )(x2, ret, u, u, u, _pool_inv_counts(seq_len), ln_g, ln_b, w_pool_bf, pool_scale, w_o_bf,
      mix_g, mix_b, wr_hi, wr_cat, br_full)


def _expert_kernel(it_tile, it_exp, it_lo, it_hi, xs_ref, wg_ref, bg_ref, wu_ref, bu_ref,
                   wd_ref, bd_ref, ys_ref):
    i = pl.program_id(0)
    lo = it_lo[i]
    hi = it_hi[i]

    def ffn(rows):
        x_lo, x_hi = _unpack_rows(xs_ref[rows, :])
        x_lo = _bf16(x_lo)
        x_hi = _bf16(x_hi)
        gt = _dot(x_lo, wg_ref[0, :HALF, :]) + _dot(x_hi, wg_ref[0, HALF:, :]) + bg_ref[0]
        up = _dot(x_lo, wu_ref[0, :HALF, :]) + _dot(x_hi, wu_ref[0, HALF:, :]) + bu_ref[0]
        gt = jnp.minimum(gt, SWIGLU_LIMIT)
        up = jnp.clip(up, -SWIGLU_LIMIT, SWIGLU_LIMIT)
        hid = (up + 1.0) * (gt * jax.nn.sigmoid(SWIGLU_ALPHA * gt))
        return _pack_rows(_dot(_bf16(hid), wd_ref[0]) + bd_ref[0])

    @pl.when(hi > lo)
    def _():
        sub = TM_EXPERT // EXPERT_SPLIT
        y = jnp.concatenate(
            [ffn(slice(s * sub, (s + 1) * sub)) for s in range(EXPERT_SPLIT)], axis=0)
        r = lax.broadcasted_iota(jnp.int32, y.shape, 0)
        mine = (r >= lo) & (r < hi)

        @pl.when(lo == 0)
        def _():
            ys_ref[...] = jnp.where(mine, y, 0)

        @pl.when(lo > 0)
        def _():
            ys_ref[...] = jnp.where(mine, y, ys_ref[...])


def _experts(items, xs, w_gate_bf, b_gate, w_up_bf, b_up, w_down_bf, b_down):
    it_tile, it_exp, it_lo, it_hi = items
    p = xs.shape[0]
    tm = TM_EXPERT
    n_items = it_tile.shape[0]
    tile_map = lambda i, t, e, lo, hi: (t[i], 0)
    w_map = lambda i, t, e, lo, hi: (e[i], 0, 0)
    w_spec = pl.BlockSpec((1, D_MODEL, D_MODEL), w_map)
    b_spec = pl.BlockSpec((1, 1, D_MODEL), w_map)
    grid_spec = pltpu.PrefetchScalarGridSpec(
        num_scalar_prefetch=4,
        grid=(n_items,),
        in_specs=[pl.BlockSpec((tm, HALF), tile_map),
                  w_spec, b_spec, w_spec, b_spec, w_spec, b_spec],
        out_specs=pl.BlockSpec((tm, HALF), tile_map),
    )
    return pl.pallas_call(
        _expert_kernel,
        grid_spec=grid_spec,
        out_shape=jax.ShapeDtypeStruct((p, HALF), jnp.int32),
        compiler_params=pltpu.CompilerParams(
            dimension_semantics=("arbitrary",), vmem_limit_bytes=VMEM_LIMIT),
        name="experts",
    )(it_tile, it_exp, it_lo, it_hi, xs, w_gate_bf, b_gate, w_up_bf, b_up, w_down_bf, b_down)


def _expert_items(counts, n_rows):
    tm = TM_EXPERT
    n_tiles = n_rows // tm
    n_items = n_tiles + N_EXPERTS - 1
    ends = jnp.cumsum(counts)
    starts = ends - counts
    first_tile = starts // tm
    last_tile = jnp.maximum(ends - 1, 0) // tm
    per_exp = jnp.where(counts > 0, last_tile - first_tile + 1, 0)
    item_end = jnp.cumsum(per_exp)
    item_start = item_end - per_exp
    total = item_end[-1]
    i = jnp.arange(n_items, dtype=jnp.int32)
    ic = jnp.minimum(i, total - 1)
    e = jnp.sum((ic[:, None] >= item_end[None, :]).astype(jnp.int32), axis=1)
    e = jnp.minimum(e, N_EXPERTS - 1)
    onehot = (e[:, None] == jnp.arange(N_EXPERTS, dtype=jnp.int32)[None, :]).astype(jnp.int32)
    pick = lambda a: jnp.sum(onehot * a[None, :], axis=1)
    tile = pick(first_tile) + ic - pick(item_start)
    lo = jnp.maximum(pick(starts), tile * tm) - tile * tm
    hi = jnp.minimum(pick(ends), (tile + 1) * tm) - tile * tm
    valid = i < total
    lo = jnp.where(valid, lo, 0)
    hi = jnp.where(valid, hi, 0)
    return (tile.astype(jnp.int32), e.astype(jnp.int32), lo.astype(jnp.int32),
            hi.astype(jnp.int32))


def _combine_kernel(h2_ref, gate_ref, g_ref, b_ref, yk_ref, o_ref):
    gates = gate_ref[...]
    h2 = h2_ref[...]
    f_lo = jnp.zeros((h2.shape[0], HALF), jnp.float32)
    f_hi = jnp.zeros((h2.shape[0], HALF), jnp.float32)
    for kk in range(TOP_K):
        y_lo, y_hi = _unpack_rows(yk_ref[kk])
        gk = gates[:, kk:kk + 1]
        f_lo = f_lo + gk * y_lo
        f_hi = f_hi + gk * y_hi
    y = DN_ALPHA * h2 + jnp.concatenate([f_lo, f_hi], axis=1)
    o_ref[...] = _layer_norm(y, g_ref[...], b_ref[...])


def _combine(h2, gates, ffn_g, ffn_b, yk):
    n = h2.shape[0]
    tm = TM_COMBINE
    row = lambda i: (i, 0)
    const = lambda i: (0, 0)
    return pl.pallas_call(
        _combine_kernel,
        grid=(n // tm,),
        in_specs=[
            pl.BlockSpec((tm, D_MODEL), row),
            pl.BlockSpec((tm, 128), row),
            pl.BlockSpec((1, D_MODEL), const),
            pl.BlockSpec((1, D_MODEL), const),
            pl.BlockSpec((TOP_K, tm, HALF), lambda i: (0, i, 0)),
        ],
        out_specs=pl.BlockSpec((tm, D_MODEL), row),
        out_shape=jax.ShapeDtypeStruct((n, D_MODEL), jnp.float32),
        compiler_params=pltpu.CompilerParams(
            dimension_semantics=("arbitrary",), vmem_limit_bytes=VMEM_LIMIT),
        name="combine",
    )(h2, gates, ffn_g, ffn_b, yk)


SC_CORES = 2
SC_SUBCORES = 16
SC_CHUNK = 64


def _chunked_slots(slots, ch):
    n = slots.shape[1]
    return slots.reshape(TOP_K, n // ch, ch).transpose(1, 0, 2)


def _sc_dispatch(slots, h2):
    n, w = h2.shape
    ch = SC_CHUNK
    n_chunks = n // (SC_CORES * SC_SUBCORES * ch)
    mesh = plsc.VectorSubcoreMesh(core_axis_name="c", subcore_axis_name="s")

    @functools.partial(
        pl.kernel, mesh=mesh,
        out_type=jax.ShapeDtypeStruct((n * TOP_K, w), h2.dtype),
        scratch_types=[pltpu.VMEM((TOP_K, ch), jnp.int32), pltpu.VMEM((ch, w), h2.dtype)],
        name="sc_dispatch")
    def k(idx_hbm, h2_hbm, xs_hbm, idx_v, rows_v):
        wid = lax.axis_index("s") * SC_CORES + lax.axis_index("c")

        @pl.loop(0, n_chunks)
        def _(c):
            chunk = wid * n_chunks + c
            pltpu.sync_copy(idx_hbm.at[chunk], idx_v)
            pltpu.sync_copy(h2_hbm.at[pl.ds(chunk * ch, ch)], rows_v)
            for kk in range(TOP_K):
                pltpu.sync_copy(rows_v, xs_hbm.at[idx_v.at[kk]])

    return k(_chunked_slots(slots, ch), h2)


def _sc_gather(slots, ys):
    n = slots.shape[1]
    w = ys.shape[1]
    ch = SC_CHUNK
    n_chunks = n // (SC_CORES * SC_SUBCORES * ch)
    mesh = plsc.VectorSubcoreMesh(core_axis_name="c", subcore_axis_name="s")

    @functools.partial(
        pl.kernel, mesh=mesh,
        out_type=jax.ShapeDtypeStruct((TOP_K, n, w), ys.dtype),
        scratch_types=[pltpu.VMEM((TOP_K, ch), jnp.int32), pltpu.VMEM((ch, w), ys.dtype)],
        name="sc_gather")
    def k(idx_hbm, ys_hbm, out_hbm, idx_v, rows_v):
        wid = lax.axis_index("s") * SC_CORES + lax.axis_index("c")

        @pl.loop(0, n_chunks)
        def _(c):
            chunk = wid * n_chunks + c
            pltpu.sync_copy(idx_hbm.at[chunk], idx_v)
            for kk in range(TOP_K):
                pltpu.sync_copy(ys_hbm.at[idx_v.at[kk]], rows_v)
                pltpu.sync_copy(rows_v, out_hbm.at[kk, pl.ds(chunk * ch, ch)])

    return k(_chunked_slots(slots, ch), ys)


def _rotary_tables(seq_len):
    d = HEAD_DIM
    inv_freq = ROPE_BASE ** (-jnp.arange(0, d, 2, dtype=jnp.float32) / d)
    ang = jnp.arange(seq_len, dtype=jnp.float32)[:, None] * inv_freq[None, :]
    cos = jnp.cos(ang)
    sin = jnp.sin(ang)
    return jnp.concatenate([cos, cos], axis=1), jnp.concatenate([-sin, sin], axis=1)


def _trunk(x, p):
    bsz, seq_len, _ = x.shape
    n = bsz * seq_len
    x2 = x.reshape(n, D_MODEL)
    cos_t, sin_t = _rotary_tables(seq_len)

    q, k, v, sg, u = _ln_proj(x2, p["ln_in_g"], p["ln_in_b"], p["w_in"], cos_t, sin_t, seq_len)
    shp = (bsz, seq_len, RET_WIDTH)
    ret = _retention(q.reshape(shp), k.reshape(shp), v.reshape(shp), sg.reshape(shp),
                     p["dl_f"], p["dl_b"]).reshape(n, RET_WIDTH)

    h2, h2p, idx8, rank8, gates, cnt = _mix_router(
        x2, ret, u, p["ln_in_g"], p["ln_in_b"], p["w_pool"], p["pool_scale"], p["w_o"],
        p["ln_mix_g"], p["ln_mix_b"], p["wr_hi"], p["wr_cat"], p["br_full"], seq_len)

    counts = cnt[:, 0]
    starts = jnp.cumsum(counts) - counts
    idx = idx8[:TOP_K]
    eq = idx[None, :, :] == jnp.arange(N_EXPERTS, dtype=jnp.int32)[:, None, None]
    slots = rank8[:TOP_K] + jnp.sum(jnp.where(eq, starts[:, None, None], 0), axis=0)

    xs = _sc_dispatch(slots, h2p)
    items = _expert_items(counts, n * TOP_K)
    ys = _experts(items, xs, p["w_gate"], p["b_gate"], p["w_up"], p["b_up"],
                  p["w_down"], p["b_down"])
    yk = _sc_gather(slots, ys)
    out = _combine(h2, gates, p["ln_ffn_g"], p["ln_ffn_b"], yk)
    return out.reshape(bsz, seq_len, D_MODEL)


def kernel(x_prompt, x_sample, ln_in_g, ln_in_b, w_in, decay_logit_fwd, decay_logit_bwd, w_pool,
           pool_scale, w_o, ln_mix_g, ln_mix_b, w_router, b_router, w_gate, b_gate, w_up, b_up,
           w_down, b_down, ln_ffn_g, ln_ffn_b):
    row = lambda a: a.reshape(1, -1)
    wr = w_router[0]
    wr_hi = _bf16(wr)
    wr_lo = _bf16(wr - wr_hi.astype(jnp.float32))
    lane_pad = jnp.zeros((D_MODEL, 128 - 2 * N_EXPERTS), jnp.bfloat16)
    wr_cat = jnp.concatenate([wr_hi, wr_lo, lane_pad], axis=1)
    wr_hi = jnp.concatenate([wr_hi, jnp.zeros_like(wr_hi), lane_pad], axis=1)
    eye = jnp.eye(len(POOL_WINDOWS), dtype=jnp.float32)
    w_pool_bd = (eye[:, None, :, None] * w_pool[0][:, :, None, :]).reshape(POOL_WIDTH, POOL_WIDTH)
    p = {
        "ln_in_g": row(ln_in_g), "ln_in_b": row(ln_in_b),
        "w_in": _bf16(w_in[0]),
        "dl_f": decay_logit_fwd[0], "dl_b": decay_logit_bwd[0],
        "w_pool": _bf16(w_pool_bd), "pool_scale": row(pool_scale[0]),
        "w_o": _bf16(w_o[0]),
        "ln_mix_g": row(ln_mix_g[0]), "ln_mix_b": row(ln_mix_b[0]),
        "wr_hi": wr_hi, "wr_cat": wr_cat,
        "br_full": jnp.broadcast_to(b_router[0][:, None], (N_EXPERTS, MIX_SUB)),
        "w_gate": _bf16(w_gate[0]), "b_gate": b_gate[0][:, None, :],
        "w_up": _bf16(w_up[0]), "b_up": b_up[0][:, None, :],
        "w_down": _bf16(w_down[0]), "b_down": b_down[0][:, None, :],
        "ln_ffn_g": row(ln_ffn_g[0]), "ln_ffn_b": row(ln_ffn_b[0]),
    }
    return (_trunk(x_prompt, p), _trunk(x_sample, p))
```

```python
import functools

import jax
import jax.numpy as jnp
import numpy as np
from jax import lax
from jax.experimental import pallas as pl
from jax.experimental.pallas import tpu as pltpu
from jax.experimental.pallas import tpu_sc as plsc

D_MODEL = 1024
RET_WIDTH = 512
POOL_WIDTH = 512
N_HEADS = 4
HEAD_DIM = 128
ROPE_BASE = 10000.0
POOL_WINDOWS = (2, 4, 8, 16)
POOL_GROUP = 128
IN_WIDTH = 4 * RET_WIDTH + POOL_WIDTH
N_EXPERTS = 32
TOP_K = 4
SWIGLU_ALPHA = 1.702
SWIGLU_LIMIT = 7.0
LN_EPS = 1e-5
DN_ALPHA = 2.0 ** 0.25

TM_PROJ = 512
RET_TILE = 1024
RET_CHUNK = 256
TM_MIX = 512
MIX_SUB = 512
POOL_HALO = 16
TM_EXPERT = 512
TM_COMBINE = 512

VMEM_LIMIT = 56 * 1024 * 1024

_NT = (((1,), (1,)), ((), ()))
_TN = (((0,), (0,)), ((), ()))


def _layer_norm(x, g, b):
    mu = jnp.mean(x, axis=-1, keepdims=True)
    xc = x - mu
    var = jnp.mean(xc * xc, axis=-1, keepdims=True)
    return xc * lax.rsqrt(var + LN_EPS) * g + b


def _bf16(x):
    return x.astype(jnp.bfloat16)


def _dot(a, b):
    return jnp.dot(a, b, preferred_element_type=jnp.float32)


HALF = D_MODEL // 2


def _pack_rows(x):
    lo = lax.bitcast_convert_type(_bf16(x[:, :HALF]).astype(jnp.float32), jnp.uint32)
    hi = lax.bitcast_convert_type(_bf16(x[:, HALF:]).astype(jnp.float32), jnp.uint32)
    return lax.bitcast_convert_type((lo >> 16) | hi, jnp.int32)


def _unpack_rows(w):
    u = lax.bitcast_convert_type(w, jnp.uint32)
    lo = lax.bitcast_convert_type(u << 16, jnp.float32)
    hi = lax.bitcast_convert_type(u & jnp.uint32(0xFFFF0000), jnp.float32)
    return lo, hi


def _ln_proj_kernel(x_ref, g_ref, b_ref, w_ref, cos_ref, sin_ref,
                    q_ref, k_ref, v_ref, sg_ref, u_ref):
    h = _bf16(_layer_norm(x_ref[...], g_ref[...], b_ref[...]))
    cos = cos_ref[...]
    sin = sin_ref[...]

    def rotary(t):
        return t * cos + pltpu.roll(t, HEAD_DIM // 2, axis=1) * sin

    R = RET_WIDTH
    pq = _dot(h, w_ref[:, 0:R])
    for hd in range(N_HEADS):
        sl = slice(hd * HEAD_DIM, (hd + 1) * HEAD_DIM)
        q_ref[:, sl] = _bf16(rotary(pq[:, sl]))
    pk = _dot(h, w_ref[:, R:2 * R])
    for hd in range(N_HEADS):
        sl = slice(hd * HEAD_DIM, (hd + 1) * HEAD_DIM)
        k_ref[:, sl] = _bf16(rotary(pk[:, sl]) * (HEAD_DIM ** -0.5))
    v_ref[...] = _bf16(_dot(h, w_ref[:, 2 * R:3 * R]))
    pg = _dot(h, w_ref[:, 3 * R:4 * R])
    sg_ref[...] = _bf16(pg * jax.nn.sigmoid(pg))
    u_ref[...] = _bf16(_dot(h, w_ref[:, 4 * R:]))


def _ln_proj(x2, ln_g, ln_b, w_in_bf, cos_t, sin_t, seq_len):
    n = x2.shape[0]
    tm = TM_PROJ
    tiles_per_seq = seq_len // tm
    row = lambda i: (i, 0)
    const = lambda i: (0, 0)
    out_sd = jax.ShapeDtypeStruct((n, RET_WIDTH), jnp.bfloat16)
    return pl.pallas_call(
        _ln_proj_kernel,
        grid=(n // tm,),
        in_specs=[
            pl.BlockSpec((tm, D_MODEL), row),
            pl.BlockSpec((1, D_MODEL), const),
            pl.BlockSpec((1, D_MODEL), const),
            pl.BlockSpec((D_MODEL, IN_WIDTH), const),
            pl.BlockSpec((tm, HEAD_DIM), lambda i: (i % tiles_per_seq, 0)),
            pl.BlockSpec((tm, HEAD_DIM), lambda i: (i % tiles_per_seq, 0)),
        ],
        out_specs=[pl.BlockSpec((tm, RET_WIDTH), row)] * 5,
        out_shape=[out_sd] * 5,
        compiler_params=pltpu.CompilerParams(
            dimension_semantics=("arbitrary",), vmem_limit_bytes=VMEM_LIMIT),
        name="ln_proj",
    )(x2, ln_g, ln_b, w_in_bf, cos_t, sin_t)


def _log_sigmoid(x):
    return jnp.minimum(x, 0.0) - jnp.log(1.0 + jnp.exp(-jnp.abs(x)))


def _retention_kernel(dlf_ref, dlb_ref, q_ref, k_ref, v_ref, sg_ref, o_ref,
                      dmat, xi_f, xi_b, zeta_f, zeta_b, cdec, s_f, s_b, snap):
    b = pl.program_id(0)
    phase = pl.program_id(1)
    j = pl.program_id(2)
    n_tiles = pl.num_programs(2)
    C = RET_CHUNK
    sub = RET_TILE // C

    @pl.when((b == 0) & (phase == 0) & (j == 0))
    def _init_tables():
        ri = lax.broadcasted_iota(jnp.int32, (C, C), 0).astype(jnp.float32)
        ci = lax.broadcasted_iota(jnp.int32, (C, C), 1).astype(jnp.float32)
        rel = ri - ci
        pos = lax.broadcasted_iota(jnp.int32, (C, HEAD_DIM), 0).astype(jnp.float32)
        zero_cc = jnp.zeros((C, C), jnp.float32)
        zero_cd = jnp.zeros((C, HEAD_DIM), jnp.float32)
        zero_dd = jnp.zeros((HEAD_DIM, HEAD_DIM), jnp.float32)
        for hd in range(N_HEADS):
            dlf = dlf_ref[hd]
            dlb = dlb_ref[hd]
            dmat[hd] = jnp.where(rel >= 0, jnp.exp(_log_sigmoid(zero_cc + dlf) * rel),
                                 jnp.exp(-_log_sigmoid(zero_cc + dlb) * rel))
            lf = _log_sigmoid(zero_cd + dlf)
            lb = _log_sigmoid(zero_cd + dlb)
            xi_f[hd] = jnp.exp(lf * (pos + 1.0))
            xi_b[hd] = jnp.exp(lb * (C - pos))
            zeta_f[hd] = jnp.exp(lf * (C - 1.0 - pos))
            zeta_b[hd] = jnp.exp(lb * pos)
            cdec[hd, 0] = jnp.exp(_log_sigmoid(zero_dd + dlf) * float(C))
            cdec[hd, 1] = jnp.exp(_log_sigmoid(zero_dd + dlb) * float(C))

    @pl.when(j == 0)
    def _reset_state():
        @pl.when(phase == 0)
        def _():
            s_b[...] = jnp.zeros_like(s_b)

        @pl.when(phase == 1)
        def _():
            s_f[...] = jnp.zeros_like(s_f)

    @pl.when(phase == 0)
    def _backward_states():
        tile = n_tiles - 1 - j

        def body(cc, carry):
            c = sub - 1 - cc
            r0 = pl.multiple_of(c * C, C)
            chunk = tile * sub + c
            for hd in range(N_HEADS):
                sl = slice(hd * HEAD_DIM, (hd + 1) * HEAD_DIM)
                kc = k_ref[0, pl.ds(r0, C), sl]
                vc = v_ref[0, pl.ds(r0, C), sl].astype(jnp.float32)
                st = s_b[hd]
                snap[chunk * N_HEADS + hd] = _bf16(st)
                kv = lax.dot_general(kc, _bf16(vc * zeta_b[hd]), _TN,
                                     preferred_element_type=jnp.float32)
                s_b[hd] = st * cdec[hd, 1] + kv
            return carry

        lax.fori_loop(0, sub, body, 0, unroll=True)

    @pl.when(phase == 1)
    def _forward_outputs():
        def body(c, carry):
            r0 = pl.multiple_of(c * C, C)
            chunk = j * sub + c
            for hd in range(N_HEADS):
                sl = slice(hd * HEAD_DIM, (hd + 1) * HEAD_DIM)
                qc = q_ref[0, pl.ds(r0, C), sl]
                kc = k_ref[0, pl.ds(r0, C), sl]
                vc = v_ref[0, pl.ds(r0, C), sl]
                scores = lax.dot_general(qc, kc, _NT, preferred_element_type=jnp.float32)
                o = _dot(_bf16(scores * dmat[hd]), vc)
                st = s_f[hd]
                both = jnp.concatenate([_bf16(st), snap[chunk * N_HEADS + hd]], axis=1)
                cross = _dot(qc, both)
                o = o + xi_f[hd] * cross[:, :HEAD_DIM] + xi_b[hd] * cross[:, HEAD_DIM:]
                kv = lax.dot_general(kc, _bf16(vc.astype(jnp.float32) * zeta_f[hd]), _TN,
                                     preferred_element_type=jnp.float32)
                s_f[hd] = st * cdec[hd, 0] + kv
                mu = jnp.mean(o, axis=-1, keepdims=True)
                oc = o - mu
                on = oc * lax.rsqrt(jnp.mean(oc * oc, axis=-1, keepdims=True) + LN_EPS)
                sg = sg_ref[0, pl.ds(r0, C), sl].astype(jnp.float32)
                o_ref[0, pl.ds(r0, C), sl] = _bf16(sg * on)
            return carry

        lax.fori_loop(0, sub, body, 0, unroll=True)


def _retention(q, k, v, sg, dl_f, dl_b):
    bsz, seq_len, _ = q.shape
    n_tiles = seq_len // RET_TILE
    n_chunks = seq_len // RET_CHUNK
    C = RET_CHUNK

    def kv_map(b, p, j):
        return (b, jnp.where(p == 0, n_tiles - 1 - j, j), 0)

    def q_map(b, p, j):
        return (b, jnp.where(p == 0, 0, j), 0)

    blk = (1, RET_TILE, RET_WIDTH)
    smem = pl.BlockSpec(memory_space=pltpu.SMEM)
    return pl.pallas_call(
        _retention_kernel,
        grid=(bsz, 2, n_tiles),
        in_specs=[smem, smem,
                  pl.BlockSpec(blk, q_map), pl.BlockSpec(blk, kv_map),
                  pl.BlockSpec(blk, kv_map), pl.BlockSpec(blk, q_map)],
        out_specs=pl.BlockSpec(blk, q_map),
        out_shape=jax.ShapeDtypeStruct(q.shape, jnp.bfloat16),
        scratch_shapes=[
            pltpu.VMEM((N_HEADS, C, C), jnp.float32),
            pltpu.VMEM((N_HEADS, C, HEAD_DIM), jnp.float32),
            pltpu.VMEM((N_HEADS, C, HEAD_DIM), jnp.float32),
            pltpu.VMEM((N_HEADS, C, HEAD_DIM), jnp.float32),
            pltpu.VMEM((N_HEADS, C, HEAD_DIM), jnp.float32),
            pltpu.VMEM((N_HEADS, 2, HEAD_DIM, HEAD_DIM), jnp.float32),
            pltpu.VMEM((N_HEADS, HEAD_DIM, HEAD_DIM), jnp.float32),
            pltpu.VMEM((N_HEADS, HEAD_DIM, HEAD_DIM), jnp.float32),
            pltpu.VMEM((n_chunks * N_HEADS, HEAD_DIM, HEAD_DIM), jnp.bfloat16),
        ],
        compiler_params=pltpu.CompilerParams(
            dimension_semantics=("arbitrary", "arbitrary", "arbitrary"),
            vmem_limit_bytes=VMEM_LIMIT),
        name="retention",
    )(dl_f, dl_b, q, k, v, sg)


def _mix_router_kernel(seq_len, x_ref, ret_ref, u_ref, up_ref, un_ref, inv_ref,
                       lng_ref, lnb_ref, wpool_ref, pscale_ref, wo_ref,
                       mg_ref, mb_ref, wr_hi_ref, wr_cat_ref, br_ref,
                       h2_ref, h2p_ref, idx_ref, rank_ref, gate_ref, cnt_ref,
                       tri, base):
    i = pl.program_id(0)
    tm = TM_MIX
    sb = MIX_SUB
    n_sub = tm // sb
    halo = POOL_HALO
    tiles_per_seq = seq_len // tm
    ts = i % tiles_per_seq

    @pl.when(i == 0)
    def _init():
        tr = lax.broadcasted_iota(jnp.int32, (sb, sb), 0)
        tc = lax.broadcasted_iota(jnp.int32, (sb, sb), 1)
        tri[...] = jnp.where(tr < tc, 1.0, 0.0).astype(jnp.bfloat16)
        base[...] = jnp.zeros_like(base)

    prev = jnp.where(ts == 0, jnp.zeros_like(up_ref[...]), up_ref[...])
    nxt = jnp.where(ts == tiles_per_seq - 1, jnp.zeros_like(un_ref[...]), un_ref[...])
    e_iota = lax.broadcasted_iota(jnp.int32, (N_EXPERTS, sb), 0)
    k_iota = lax.broadcasted_iota(jnp.int32, (8, sb), 0)
    wr_hi = wr_hi_ref[...]
    wr_cat = wr_cat_ref[...]
    running = base[...]

    for s in range(n_sub):
        r0 = s * sb
        rows = slice(r0, r0 + sb)
        u_cur = u_ref[rows, :]
        above = prev if s == 0 else u_ref[r0 - halo:r0, :]
        below = nxt if s == n_sub - 1 else u_ref[r0 + sb:r0 + sb + halo, :]
        u_ext = jnp.concatenate([above, u_cur, below], axis=0).astype(jnp.float32)
        n_ext = sb + 2 * halo

        sums = []
        for gi, w in enumerate(POOL_WINDOWS):
            a = u_ext[:, gi * POOL_GROUP:(gi + 1) * POOL_GROUP]
            span = 1
            while span < w // 2:
                a = a + pltpu.roll(a, n_ext - span, axis=0)
                span *= 2
            sums.append(a + pltpu.roll(a, w // 2, axis=0))
        wsum = jnp.concatenate(sums, axis=1)[halo:halo + sb, :]
        pooled = wsum * inv_ref[0, rows, :] - u_cur.astype(jnp.float32)
        pool = _bf16(_dot(_bf16(pooled), wpool_ref[...]) * pscale_ref[...])

        m = _dot(ret_ref[rows, :], wo_ref[0:RET_WIDTH, :]) + _dot(pool, wo_ref[RET_WIDTH:, :])
        h = _layer_norm(x_ref[rows, :], lng_ref[...], lnb_ref[...])
        h2 = _layer_norm(DN_ALPHA * h + m, mg_ref[...], mb_ref[...])
        h2_ref[rows, :] = h2
        h2p_ref[rows, :] = _pack_rows(h2)

        h_hi = _bf16(h2)
        h_lo = _bf16(h2 - h_hi.astype(jnp.float32))
        part = _dot(h_hi, wr_cat)
        lrow = part + pltpu.roll(part, 128 - N_EXPERTS, axis=1) + _dot(h_lo, wr_hi)
        logits = lrow.T[0:N_EXPERTS, :] + br_ref[...]

        vals8 = jnp.zeros((8, sb), jnp.float32)
        idx8 = jnp.zeros((8, sb), jnp.int32)
        sels = []
        work = logits
        for kk in range(TOP_K):
            mval = jnp.max(work, axis=0, keepdims=True)
            midx = jnp.min(jnp.where(work == mval, e_iota, N_EXPERTS), axis=0, keepdims=True)
            sel = e_iota == midx
            sels.append(sel)
            vals8 = jnp.where(k_iota == kk, mval, vals8)
            idx8 = jnp.where(k_iota == kk, midx, idx8)
            work = jnp.where(sel, -jnp.inf, work)

        ex = jnp.where(k_iota < TOP_K, jnp.exp(vals8 - vals8[0:1, :]), 0.0)
        gates8 = ex / jnp.sum(ex, axis=0, keepdims=True)

        onehot = jnp.zeros((N_EXPERTS, sb), jnp.float32)
        for sel in sels:
            onehot = onehot + jnp.where(sel, 1.0, 0.0)
        cum = _dot(_bf16(onehot), tri[...])
        before = running + cum
        rank8 = jnp.zeros((8, sb), jnp.int32)
        for kk, sel in enumerate(sels):
            rk = jnp.sum(jnp.where(sel, before, 0.0), axis=0, keepdims=True).astype(jnp.int32)
            rank8 = jnp.where(k_iota == kk, rk, rank8)
        total = cum[:, sb - 1:sb] + onehot[:, sb - 1:sb]
        running = running + jnp.broadcast_to(total, (N_EXPERTS, sb))

        idx_ref[:, rows] = idx8
        rank_ref[:, rows] = rank8
        gpad = jnp.concatenate([gates8, jnp.zeros((128 - 8, sb), jnp.float32)], axis=0)
        gate_ref[rows, :] = gpad.T

    base[...] = running
    cnt_ref[...] = running[:, 0:128].astype(jnp.int32)


def _pool_inv_counts(seq_len):
    tm = TM_MIX
    assert seq_len // tm >= 3
    cols = []
    for w in POOL_WINDOWS:
        pos = np.arange(seq_len)
        lo = np.clip(pos - w // 2, 0, seq_len - 1)
        hi = np.clip(pos + (w - 1 - w // 2), 0, seq_len - 1)
        inv = (1.0 / (hi - lo + 1)).astype(np.float32)
        kinds = np.stack([inv[:tm], inv[tm:2 * tm], inv[seq_len - tm:]])
        cols.append(np.repeat(kinds[:, :, None], POOL_GROUP, axis=2))
    return jnp.asarray(np.concatenate(cols, axis=2))


def _mix_router(x2, ret, u, ln_g, ln_b, w_pool_bf, pool_scale, w_o_bf, mix_g, mix_b,
                wr_hi, wr_cat, br_full, seq_len):
    n = x2.shape[0]
    tm = TM_MIX
    sb = MIX_SUB
    halo = POOL_HALO
    hb = tm // halo
    n_halo_blocks = n // halo
    tiles_per_seq = seq_len // tm
    row = lambda i: (i, 0)
    const = lambda i: (0, 0)
    col = lambda i: (0, i)

    def inv_map(i):
        ts = i % tiles_per_seq
        return (jnp.where(ts == 0, 0, jnp.where(ts == tiles_per_seq - 1, 2, 1)), 0, 0)

    return pl.pallas_call(
        functools.partial(_mix_router_kernel, seq_len),
        grid=(n // tm,),
        in_specs=[
            pl.BlockSpec((tm, D_MODEL), row),
            pl.BlockSpec((tm, RET_WIDTH), row),
            pl.BlockSpec((tm, POOL_WIDTH), row),
            pl.BlockSpec((halo, POOL_WIDTH), lambda i: (jnp.maximum(i * hb - 1, 0), 0)),
            pl.BlockSpec((halo, POOL_WIDTH),
                         lambda i: (jnp.minimum((i + 1) * hb, n_halo_blocks - 1), 0)),
            pl.BlockSpec((1, tm, POOL_WIDTH), inv_map),
            pl.BlockSpec((1, D_MODEL), const),
            pl.BlockSpec((1, D_MODEL), const),
            pl.BlockSpec((POOL_WIDTH, POOL_WIDTH), const),
            pl.BlockSpec((1, POOL_WIDTH), const),
            pl.BlockSpec((D_MODEL, D_MODEL), const),
            pl.BlockSpec((1, D_MODEL), const),
            pl.BlockSpec((1, D_MODEL), const),
            pl.BlockSpec((D_MODEL, 128), const),
            pl.BlockSpec((D_MODEL, 128), const),
            pl.BlockSpec((N_EXPERTS, sb), const),
        ],
        out_specs=[
            pl.BlockSpec((tm, D_MODEL), row),
            pl.BlockSpec((tm, HALF), row),
            pl.BlockSpec((8, tm), col),
            pl.BlockSpec((8, tm), col),
            pl.BlockSpec((tm, 128), row),
            pl.BlockSpec((N_EXPERTS, 128), const),
        ],
        out_shape=[
            jax.ShapeDtypeStruct((n, D_MODEL), jnp.float32),
            jax.ShapeDtypeStruct((n, HALF), jnp.int32),
            jax.ShapeDtypeStruct((8, n), jnp.int32),
            jax.ShapeDtypeStruct((8, n), jnp.int32),
            jax.ShapeDtypeStruct((n, 128), jnp.float32),
            jax.ShapeDtypeStruct((N_EXPERTS, 128), jnp.int32),
        ],
        scratch_shapes=[
            pltpu.VMEM((sb, sb), jnp.bfloat16),
            pltpu.VMEM((N_EXPERTS, sb), jnp.float32),
        ],
        compiler_params=pltpu.CompilerParams(
            dimension_semantics=("arbitrary",), vmem_limit_bytes=VMEM_LIMIT),
        name="mix_router",
    )(x2, ret, u, u, u, _pool_inv_counts(seq_len), ln_g, ln_b, w_pool_bf, pool_scale, w_o_bf,
      mix_g, mix_b, wr_hi, wr_cat, br_full)


def _expert_kernel(it_tile, it_exp, it_rows, it_new, xs_ref, wg_ref, bg_ref, wu_ref, bu_ref,
                   wd_ref, bd_ref, ys_ref, wg_bf, wu_bf, wd_bf):
    i = pl.program_id(0)
    n_valid = it_rows[i]

    @pl.when(it_new[i] == 1)
    def _cast_weights():
        wg_bf[...] = _bf16(wg_ref[0])
        wu_bf[...] = _bf16(wu_ref[0])
        wd_bf[...] = _bf16(wd_ref[0])

    @pl.when(n_valid > 0)
    def _ffn():
        r = lax.broadcasted_iota(jnp.int32, (TM_EXPERT, HALF), 0)
        x_lo, x_hi = _unpack_rows(jnp.where(r < n_valid, xs_ref[...], 0))
        x_lo = _bf16(x_lo)
        x_hi = _bf16(x_hi)
        gt = _dot(x_lo, wg_bf[:HALF, :]) + _dot(x_hi, wg_bf[HALF:, :]) + bg_ref[0]
        up = _dot(x_lo, wu_bf[:HALF, :]) + _dot(x_hi, wu_bf[HALF:, :]) + bu_ref[0]
        gt = jnp.minimum(gt, SWIGLU_LIMIT)
        up = jnp.clip(up, -SWIGLU_LIMIT, SWIGLU_LIMIT)
        hid = (up + 1.0) * (gt * jax.nn.sigmoid(SWIGLU_ALPHA * gt))
        ys_ref[...] = _pack_rows(_dot(_bf16(hid), wd_bf[...]) + bd_ref[0])


def _experts(items, xs, w_gate, b_gate, w_up, b_up, w_down, b_down):
    it_tile, it_exp, it_rows, it_new = items
    p = xs.shape[0]
    tm = TM_EXPERT
    n_items = it_tile.shape[0]
    tile_map = lambda i, t, e, rows, nw: (t[i], 0)
    w_map = lambda i, t, e, rows, nw: (e[i], 0, 0)
    w_spec = pl.BlockSpec((1, D_MODEL, D_MODEL), w_map)
    b_spec = pl.BlockSpec((1, 1, D_MODEL), w_map)
    w_scratch = pltpu.VMEM((D_MODEL, D_MODEL), jnp.bfloat16)
    grid_spec = pltpu.PrefetchScalarGridSpec(
        num_scalar_prefetch=4,
        grid=(n_items,),
        in_specs=[pl.BlockSpec((tm, HALF), tile_map),
                  w_spec, b_spec, w_spec, b_spec, w_spec, b_spec],
        out_specs=pl.BlockSpec((tm, HALF), tile_map),
        scratch_shapes=[w_scratch, w_scratch, w_scratch],
    )
    return pl.pallas_call(
        _expert_kernel,
        grid_spec=grid_spec,
        out_shape=jax.ShapeDtypeStruct((p, HALF), jnp.int32),
        compiler_params=pltpu.CompilerParams(
            dimension_semantics=("arbitrary",), vmem_limit_bytes=VMEM_LIMIT),
        name="experts",
    )(it_tile, it_exp, it_rows, it_new, xs, w_gate, b_gate, w_up, b_up, w_down, b_down)


def _padded_tiles(n_rows):
    return n_rows // TM_EXPERT + N_EXPERTS


def _expert_layout(counts, n_rows):
    tm = TM_EXPERT
    n_items = _padded_tiles(n_rows)
    tiles_e = (counts + tm - 1) // tm
    tile_end = jnp.cumsum(tiles_e)
    tile_start = tile_end - tiles_e
    total = tile_end[-1]
    i = jnp.arange(n_items, dtype=jnp.int32)
    tile = jnp.minimum(i, total - 1)
    e = jnp.sum((tile[:, None] >= tile_end[None, :]).astype(jnp.int32), axis=1)
    e = jnp.minimum(e, N_EXPERTS - 1)
    onehot = (e[:, None] == jnp.arange(N_EXPERTS, dtype=jnp.int32)[None, :]).astype(jnp.int32)
    pick = lambda a: jnp.sum(onehot * a[None, :], axis=1)
    first = pick(tile_start)
    valid = i < total
    rows = jnp.where(valid, jnp.clip(pick(counts) - (tile - first) * tm, 0, tm), 0)
    new = valid & (tile == first)
    items = (tile.astype(jnp.int32), e.astype(jnp.int32), rows.astype(jnp.int32),
             new.astype(jnp.int32))
    return tile_start * tm, items


def _combine_kernel(h2_ref, gate_ref, g_ref, b_ref, yk_ref, o_ref):
    gates = gate_ref[...]
    h2 = h2_ref[...]
    f_lo = jnp.zeros((h2.shape[0], HALF), jnp.float32)
    f_hi = jnp.zeros((h2.shape[0], HALF), jnp.float32)
    for kk in range(TOP_K):
        y_lo, y_hi = _unpack_rows(yk_ref[kk])
        gk = gates[:, kk:kk + 1]
        f_lo = f_lo + gk * y_lo
        f_hi = f_hi + gk * y_hi
    y = DN_ALPHA * h2 + jnp.concatenate([f_lo, f_hi], axis=1)
    o_ref[...] = _layer_norm(y, g_ref[...], b_ref[...])


def _combine(h2, gates, ffn_g, ffn_b, yk):
    n = h2.shape[0]
    tm = TM_COMBINE
    row = lambda i: (i, 0)
    const = lambda i: (0, 0)
    return pl.pallas_call(
        _combine_kernel,
        grid=(n // tm,),
        in_specs=[
            pl.BlockSpec((tm, D_MODEL), row),
            pl.BlockSpec((tm, 128), row),
            pl.BlockSpec((1, D_MODEL), const),
            pl.BlockSpec((1, D_MODEL), const),
            pl.BlockSpec((TOP_K, tm, HALF), lambda i: (0, i, 0)),
        ],
        out_specs=pl.BlockSpec((tm, D_MODEL), row),
        out_shape=jax.ShapeDtypeStruct((n, D_MODEL), jnp.float32),
        compiler_params=pltpu.CompilerParams(
            dimension_semantics=("arbitrary",), vmem_limit_bytes=VMEM_LIMIT),
        name="combine",
    )(h2, gates, ffn_g, ffn_b, yk)


SC_CORES = 2
SC_SUBCORES = 16
SC_CHUNK = 64


def _sc_dispatch(slots, h2, n_out_rows):
    n, w = h2.shape
    ch = SC_CHUNK
    n_chunks = n // (SC_CORES * SC_SUBCORES * ch)
    mesh = plsc.VectorSubcoreMesh(core_axis_name="c", subcore_axis_name="s")

    @functools.partial(
        pl.kernel, mesh=mesh,
        out_type=jax.ShapeDtypeStruct((n_out_rows, w), h2.dtype),
        scratch_types=[pltpu.VMEM((TOP_K, ch), jnp.int32), pltpu.VMEM((ch, w), h2.dtype)],
        name="sc_dispatch")
    def k(idx_hbm, h2_hbm, xs_hbm, idx_v, rows_v):
        wid = lax.axis_index("s") * SC_CORES + lax.axis_index("c")

        @pl.loop(0, n_chunks)
        def _(c):
            chunk = wid * n_chunks + c
            for kk in range(TOP_K):
                pltpu.sync_copy(idx_hbm.at[kk, pl.ds(chunk * ch, ch)], idx_v.at[kk])
            pltpu.sync_copy(h2_hbm.at[pl.ds(chunk * ch, ch)], rows_v)
            for kk in range(TOP_K):
                pltpu.sync_copy(rows_v, xs_hbm.at[idx_v.at[kk]])

    return k(slots, h2)


def _sc_gather(slots, ys):
    n = slots.shape[1]
    w = ys.shape[1]
    ch = SC_CHUNK
    n_chunks = n // (SC_CORES * SC_SUBCORES * ch)
    mesh = plsc.VectorSubcoreMesh(core_axis_name="c", subcore_axis_name="s")

    @functools.partial(
        pl.kernel, mesh=mesh,
        out_type=jax.ShapeDtypeStruct((TOP_K, n, w), ys.dtype),
        scratch_types=[pltpu.VMEM((TOP_K, ch), jnp.int32), pltpu.VMEM((ch, w), ys.dtype)],
        name="sc_gather")
    def k(idx_hbm, ys_hbm, out_hbm, idx_v, rows_v):
        wid = lax.axis_index("s") * SC_CORES + lax.axis_index("c")

        @pl.loop(0, n_chunks)
        def _(c):
            chunk = wid * n_chunks + c
            for kk in range(TOP_K):
                pltpu.sync_copy(idx_hbm.at[kk, pl.ds(chunk * ch, ch)], idx_v.at[kk])
            for kk in range(TOP_K):
                pltpu.sync_copy(ys_hbm.at[idx_v.at[kk]], rows_v)
                pltpu.sync_copy(rows_v, out_hbm.at[kk, pl.ds(chunk * ch, ch)])

    return k(slots, ys)


def _rotary_tables(seq_len):
    d = HEAD_DIM
    inv_freq = ROPE_BASE ** (-jnp.arange(0, d, 2, dtype=jnp.float32) / d)
    ang = jnp.arange(seq_len, dtype=jnp.float32)[:, None] * inv_freq[None, :]
    cos = jnp.cos(ang)
    sin = jnp.sin(ang)
    return jnp.concatenate([cos, cos], axis=1), jnp.concatenate([-sin, sin], axis=1)


def _trunk(x, p):
    bsz, seq_len, _ = x.shape
    n = bsz * seq_len
    x2 = x.reshape(n, D_MODEL)
    cos_t, sin_t = _rotary_tables(seq_len)

    q, k, v, sg, u = _ln_proj(x2, p["ln_in_g"], p["ln_in_b"], p["w_in"], cos_t, sin_t, seq_len)
    shp = (bsz, seq_len, RET_WIDTH)
    ret = _retention(q.reshape(shp), k.reshape(shp), v.reshape(shp), sg.reshape(shp),
                     p["dl_f"], p["dl_b"]).reshape(n, RET_WIDTH)

    h2, h2p, idx8, rank8, gates, cnt = _mix_router(
        x2, ret, u, p["ln_in_g"], p["ln_in_b"], p["w_pool"], p["pool_scale"], p["w_o"],
        p["ln_mix_g"], p["ln_mix_b"], p["wr_hi"], p["wr_cat"], p["br_full"], seq_len)

    starts, items = _expert_layout(cnt[:, 0], n * TOP_K)
    idx = idx8[:TOP_K]
    eq = idx[None, :, :] == jnp.arange(N_EXPERTS, dtype=jnp.int32)[:, None, None]
    slots = rank8[:TOP_K] + jnp.sum(jnp.where(eq, starts[:, None, None], 0), axis=0)

    xs = _sc_dispatch(slots, h2p, _padded_tiles(n * TOP_K) * TM_EXPERT)
    ys = _experts(items, xs, p["w_gate"], p["b_gate"], p["w_up"], p["b_up"],
                  p["w_down"], p["b_down"])
    yk = _sc_gather(slots, ys)
    out = _combine(h2, gates, p["ln_ffn_g"], p["ln_ffn_b"], yk)
    return out.reshape(bsz, seq_len, D_MODEL)


def kernel(x_prompt, x_sample, ln_in_g, ln_in_b, w_in, decay_logit_fwd, decay_logit_bwd, w_pool,
           pool_scale, w_o, ln_mix_g, ln_mix_b, w_router, b_router, w_gate, b_gate, w_up, b_up,
           w_down, b_down, ln_ffn_g, ln_ffn_b):
    row = lambda a: a.reshape(1, -1)
    wr = w_router[0]
    wr_hi = _bf16(wr)
    wr_lo = _bf16(wr - wr_hi.astype(jnp.float32))
    lane_pad = jnp.zeros((D_MODEL, 128 - 2 * N_EXPERTS), jnp.bfloat16)
    wr_cat = jnp.concatenate([wr_hi, wr_lo, lane_pad], axis=1)
    wr_hi = jnp.concatenate([wr_hi, jnp.zeros_like(wr_hi), lane_pad], axis=1)
    eye = jnp.eye(len(POOL_WINDOWS), dtype=jnp.float32)
    w_pool_bd = (eye[:, None, :, None] * w_pool[0][:, :, None, :]).reshape(POOL_WIDTH, POOL_WIDTH)
    p = {
        "ln_in_g": row(ln_in_g), "ln_in_b": row(ln_in_b),
        "w_in": _bf16(w_in[0]),
        "dl_f": decay_logit_fwd[0], "dl_b": decay_logit_bwd[0],
        "w_pool": _bf16(w_pool_bd), "pool_scale": row(pool_scale[0]),
        "w_o": _bf16(w_o[0]),
        "ln_mix_g": row(ln_mix_g[0]), "ln_mix_b": row(ln_mix_b[0]),
        "wr_hi": wr_hi, "wr_cat": wr_cat,
        "br_full": jnp.broadcast_to(b_router[0][:, None], (N_EXPERTS, MIX_SUB)),
        "w_gate": w_gate[0], "b_gate": b_gate[0][:, None, :],
        "w_up": w_up[0], "b_up": b_up[0][:, None, :],
        "w_down": w_down[0], "b_down": b_down[0][:, None, :],
        "ln_ffn_g": row(ln_ffn_g[0]), "ln_ffn_b": row(ln_ffn_b[0]),
    }
    return (_trunk(x_prompt, p), _trunk(x_sample, p))
```

```python
import functools

import jax
import jax.numpy as jnp
import numpy as np
from jax import lax
from jax.experimental import pallas as pl
from jax.experimental.pallas import tpu as pltpu
from jax.experimental.pallas import tpu_sc as plsc

D_MODEL = 1024
RET_WIDTH = 512
POOL_WIDTH = 512
N_HEADS = 4
HEAD_DIM = 128
ROPE_BASE = 10000.0
POOL_WINDOWS = (2, 4, 8, 16)
POOL_GROUP = 128
IN_WIDTH = 4 * RET_WIDTH + POOL_WIDTH
N_EXPERTS = 32
TOP_K = 4
SWIGLU_ALPHA = 1.702
SWIGLU_LIMIT = 7.0
LN_EPS = 1e-5
DN_ALPHA = 2.0 ** 0.25

TM_PROJ = 512
RET_TILE = 1024
RET_CHUNK = 256
TM_MIX = 512
MIX_SUB = 512
POOL_HALO = 16
TM_EXPERT = 512
TM_COMBINE = 512

VMEM_LIMIT = 56 * 1024 * 1024

_NT = (((1,), (1,)), ((), ()))
_TN = (((0,), (0,)), ((), ()))


def _layer_norm(x, g, b):
    mu = jnp.mean(x, axis=-1, keepdims=True)
    xc = x - mu
    var = jnp.mean(xc * xc, axis=-1, keepdims=True)
    return xc * lax.rsqrt(var + LN_EPS) * g + b


def _bf16(x):
    return x.astype(jnp.bfloat16)


def _dot(a, b):
    return jnp.dot(a, b, preferred_element_type=jnp.float32)


HALF = D_MODEL // 2


def _pack_rows(x):
    lo = lax.bitcast_convert_type(_bf16(x[:, :HALF]).astype(jnp.float32), jnp.uint32)
    hi = lax.bitcast_convert_type(_bf16(x[:, HALF:]).astype(jnp.float32), jnp.uint32)
    return lax.bitcast_convert_type((lo >> 16) | hi, jnp.int32)


def _unpack_rows(w):
    u = lax.bitcast_convert_type(w, jnp.uint32)
    lo = lax.bitcast_convert_type(u << 16, jnp.float32)
    hi = lax.bitcast_convert_type(u & jnp.uint32(0xFFFF0000), jnp.float32)
    return lo, hi


def _ln_proj_kernel(x_ref, g_ref, b_ref, w_ref, cos_ref, sin_ref,
                    q_ref, k_ref, v_ref, sg_ref, u_ref):
    h = _bf16(_layer_norm(x_ref[...], g_ref[...], b_ref[...]))
    cos = cos_ref[...]
    sin = sin_ref[...]

    def rotary(t):
        return t * cos + pltpu.roll(t, HEAD_DIM // 2, axis=1) * sin

    R = RET_WIDTH
    pq = _dot(h, w_ref[:, 0:R])
    for hd in range(N_HEADS):
        sl = slice(hd * HEAD_DIM, (hd + 1) * HEAD_DIM)
        q_ref[:, sl] = _bf16(rotary(pq[:, sl]))
    pk = _dot(h, w_ref[:, R:2 * R])
    for hd in range(N_HEADS):
        sl = slice(hd * HEAD_DIM, (hd + 1) * HEAD_DIM)
        k_ref[:, sl] = _bf16(rotary(pk[:, sl]) * (HEAD_DIM ** -0.5))
    v_ref[...] = _bf16(_dot(h, w_ref[:, 2 * R:3 * R]))
    pg = _dot(h, w_ref[:, 3 * R:4 * R])
    sg_ref[...] = _bf16(pg * jax.nn.sigmoid(pg))
    u_ref[...] = _bf16(_dot(h, w_ref[:, 4 * R:]))


def _ln_proj(x2, ln_g, ln_b, w_in_bf, cos_t, sin_t, seq_len):
    n = x2.shape[0]
    tm = TM_PROJ
    tiles_per_seq = seq_len // tm
    row = lambda i: (i, 0)
    const = lambda i: (0, 0)
    out_sd = jax.ShapeDtypeStruct((n, RET_WIDTH), jnp.bfloat16)
    return pl.pallas_call(
        _ln_proj_kernel,
        grid=(n // tm,),
        in_specs=[
            pl.BlockSpec((tm, D_MODEL), row),
            pl.BlockSpec((1, D_MODEL), const),
            pl.BlockSpec((1, D_MODEL), const),
            pl.BlockSpec((D_MODEL, IN_WIDTH), const),
            pl.BlockSpec((tm, HEAD_DIM), lambda i: (i % tiles_per_seq, 0)),
            pl.BlockSpec((tm, HEAD_DIM), lambda i: (i % tiles_per_seq, 0)),
        ],
        out_specs=[pl.BlockSpec((tm, RET_WIDTH), row)] * 5,
        out_shape=[out_sd] * 5,
        compiler_params=pltpu.CompilerParams(
            dimension_semantics=("arbitrary",), vmem_limit_bytes=VMEM_LIMIT),
        name="ln_proj",
    )(x2, ln_g, ln_b, w_in_bf, cos_t, sin_t)


def _log_sigmoid(x):
    return jnp.minimum(x, 0.0) - jnp.log(1.0 + jnp.exp(-jnp.abs(x)))


def _retention_kernel(dlf_ref, dlb_ref, q_ref, k_ref, v_ref, sg_ref, o_ref,
                      dmat, xi_f, xi_b, zeta_f, zeta_b, cdec, s_f, s_b, snap):
    b = pl.program_id(0)
    phase = pl.program_id(1)
    j = pl.program_id(2)
    n_tiles = pl.num_programs(2)
    C = RET_CHUNK
    sub = RET_TILE // C

    @pl.when((b == 0) & (phase == 0) & (j == 0))
    def _init_tables():
        ri = lax.broadcasted_iota(jnp.int32, (C, C), 0).astype(jnp.float32)
        ci = lax.broadcasted_iota(jnp.int32, (C, C), 1).astype(jnp.float32)
        rel = ri - ci
        pos = lax.broadcasted_iota(jnp.int32, (C, HEAD_DIM), 0).astype(jnp.float32)
        zero_cc = jnp.zeros((C, C), jnp.float32)
        zero_cd = jnp.zeros((C, HEAD_DIM), jnp.float32)
        zero_dd = jnp.zeros((HEAD_DIM, HEAD_DIM), jnp.float32)
        for hd in range(N_HEADS):
            dlf = dlf_ref[hd]
            dlb = dlb_ref[hd]
            dmat[hd] = jnp.where(rel >= 0, jnp.exp(_log_sigmoid(zero_cc + dlf) * rel),
                                 jnp.exp(-_log_sigmoid(zero_cc + dlb) * rel))
            lf = _log_sigmoid(zero_cd + dlf)
            lb = _log_sigmoid(zero_cd + dlb)
            xi_f[hd] = jnp.exp(lf * (pos + 1.0))
            xi_b[hd] = jnp.exp(lb * (C - pos))
            zeta_f[hd] = jnp.exp(lf * (C - 1.0 - pos))
            zeta_b[hd] = jnp.exp(lb * pos)
            cdec[hd, 0] = jnp.exp(_log_sigmoid(zero_dd + dlf) * float(C))
            cdec[hd, 1] = jnp.exp(_log_sigmoid(zero_dd + dlb) * float(C))

    @pl.when(j == 0)
    def _reset_state():
        @pl.when(phase == 0)
        def _():
            s_b[...] = jnp.zeros_like(s_b)

        @pl.when(phase == 1)
        def _():
            s_f[...] = jnp.zeros_like(s_f)

    @pl.when(phase == 0)
    def _backward_states():
        tile = n_tiles - 1 - j

        def body(cc, carry):
            c = sub - 1 - cc
            r0 = pl.multiple_of(c * C, C)
            chunk = tile * sub + c
            for hd in range(N_HEADS):
                sl = slice(hd * HEAD_DIM, (hd + 1) * HEAD_DIM)
                kc = k_ref[0, pl.ds(r0, C), sl]
                vc = v_ref[0, pl.ds(r0, C), sl].astype(jnp.float32)
                st = s_b[hd]
                snap[chunk * N_HEADS + hd] = _bf16(st)
                kv = lax.dot_general(kc, _bf16(vc * zeta_b[hd]), _TN,
                                     preferred_element_type=jnp.float32)
                s_b[hd] = st * cdec[hd, 1] + kv
            return carry

        lax.fori_loop(0, sub, body, 0, unroll=True)

    @pl.when(phase == 1)
    def _forward_outputs():
        def body(c, carry):
            r0 = pl.multiple_of(c * C, C)
            chunk = j * sub + c
            for hd in range(N_HEADS):
                sl = slice(hd * HEAD_DIM, (hd + 1) * HEAD_DIM)
                qc = q_ref[0, pl.ds(r0, C), sl]
                kc = k_ref[0, pl.ds(r0, C), sl]
                vc = v_ref[0, pl.ds(r0, C), sl]
                scores = lax.dot_general(qc, kc, _NT, preferred_element_type=jnp.float32)
                o = _dot(_bf16(scores * dmat[hd]), vc)
                st = s_f[hd]
                both = jnp.concatenate([_bf16(st), snap[chunk * N_HEADS + hd]], axis=1)
                cross = _dot(qc, both)
                o = o + xi_f[hd] * cross[:, :HEAD_DIM] + xi_b[hd] * cross[:, HEAD_DIM:]
                kv = lax.dot_general(kc, _bf16(vc.astype(jnp.float32) * zeta_f[hd]), _TN,
                                     preferred_element_type=jnp.float32)
                s_f[hd] = st * cdec[hd, 0] + kv
                mu = jnp.mean(o, axis=-1, keepdims=True)
                oc = o - mu
                on = oc * lax.rsqrt(jnp.mean(oc * oc, axis=-1, keepdims=True) + LN_EPS)
                sg = sg_ref[0, pl.ds(r0, C), sl].astype(jnp.float32)
                o_ref[0, pl.ds(r0, C), sl] = _bf16(sg * on)
            return carry

        lax.fori_loop(0, sub, body, 0, unroll=True)


def _retention(q, k, v, sg, dl_f, dl_b):
    bsz, seq_len, _ = q.shape
    n_tiles = seq_len // RET_TILE
    n_chunks = seq_len // RET_CHUNK
    C = RET_CHUNK

    def kv_map(b, p, j):
        return (b, jnp.where(p == 0, n_tiles - 1 - j, j), 0)

    def q_map(b, p, j):
        return (b, jnp.where(p == 0, 0, j), 0)

    blk = (1, RET_TILE, RET_WIDTH)
    smem = pl.BlockSpec(memory_space=pltpu.SMEM)
    return pl.pallas_call(
        _retention_kernel,
        grid=(bsz, 2, n_tiles),
        in_specs=[smem, smem,
                  pl.BlockSpec(blk, q_map), pl.BlockSpec(blk, kv_map),
                  pl.BlockSpec(blk, kv_map), pl.BlockSpec(blk, q_map)],
        out_specs=pl.BlockSpec(blk, q_map),
        out_shape=jax.ShapeDtypeStruct(q.shape, jnp.bfloat16),
        scratch_shapes=[
            pltpu.VMEM((N_HEADS, C, C), jnp.float32),
            pltpu.VMEM((N_HEADS, C, HEAD_DIM), jnp.float32),
            pltpu.VMEM((N_HEADS, C, HEAD_DIM), jnp.float32),
            pltpu.VMEM((N_HEADS, C, HEAD_DIM), jnp.float32),
            pltpu.VMEM((N_HEADS, C, HEAD_DIM), jnp.float32),
            pltpu.VMEM((N_HEADS, 2, HEAD_DIM, HEAD_DIM), jnp.float32),
            pltpu.VMEM((N_HEADS, HEAD_DIM, HEAD_DIM), jnp.float32),
            pltpu.VMEM((N_HEADS, HEAD_DIM, HEAD_DIM), jnp.float32),
            pltpu.VMEM((n_chunks * N_HEADS, HEAD_DIM, HEAD_DIM), jnp.bfloat16),
        ],
        compiler_params=pltpu.CompilerParams(
            dimension_semantics=("arbitrary", "arbitrary", "arbitrary"),
            vmem_limit_bytes=VMEM_LIMIT),
        name="retention",
    )(dl_f, dl_b, q, k, v, sg)


def _mix_router_kernel(seq_len, x_ref, ret_ref, u_ref, up_ref, un_ref, inv_ref,
                       lng_ref, lnb_ref, wpool_ref, pscale_ref, wo_ref,
                       mg_ref, mb_ref, wr_hi_ref, wr_cat_ref, br_ref,
                       h2_ref, h2p_ref, idx_ref, rank_ref, gate_ref, cnt_ref,
                       tri, base):
    i = pl.program_id(0)
    tm = TM_MIX
    sb = MIX_SUB
    n_sub = tm // sb
    halo = POOL_HALO
    tiles_per_seq = seq_len // tm
    ts = i % tiles_per_seq

    @pl.when(i == 0)
    def _init():
        tr = lax.broadcasted_iota(jnp.int32, (sb, sb), 0)
        tc = lax.broadcasted_iota(jnp.int32, (sb, sb), 1)
        tri[...] = jnp.where(tr < tc, 1.0, 0.0).astype(jnp.bfloat16)
        base[...] = jnp.zeros_like(base)

    prev = jnp.where(ts == 0, jnp.zeros_like(up_ref[...]), up_ref[...])
    nxt = jnp.where(ts == tiles_per_seq - 1, jnp.zeros_like(un_ref[...]), un_ref[...])
    e_iota = lax.broadcasted_iota(jnp.int32, (N_EXPERTS, sb), 0)
    k_iota = lax.broadcasted_iota(jnp.int32, (8, sb), 0)
    wr_hi = wr_hi_ref[...]
    wr_cat = wr_cat_ref[...]
    running = base[...]

    for s in range(n_sub):
        r0 = s * sb
        rows = slice(r0, r0 + sb)
        u_cur = u_ref[rows, :]
        above = prev if s == 0 else u_ref[r0 - halo:r0, :]
        below = nxt if s == n_sub - 1 else u_ref[r0 + sb:r0 + sb + halo, :]
        u_ext = jnp.concatenate([above, u_cur, below], axis=0).astype(jnp.float32)
        n_ext = sb + 2 * halo

        sums = []
        for gi, w in enumerate(POOL_WINDOWS):
            a = u_ext[:, gi * POOL_GROUP:(gi + 1) * POOL_GROUP]
            span = 1
            while span < w // 2:
                a = a + pltpu.roll(a, n_ext - span, axis=0)
                span *= 2
            sums.append(a + pltpu.roll(a, w // 2, axis=0))
        wsum = jnp.concatenate(sums, axis=1)[halo:halo + sb, :]
        pooled = wsum * inv_ref[0, rows, :] - u_cur.astype(jnp.float32)
        pool = _bf16(_dot(_bf16(pooled), wpool_ref[...]) * pscale_ref[...])

        m = _dot(ret_ref[rows, :], wo_ref[0:RET_WIDTH, :]) + _dot(pool, wo_ref[RET_WIDTH:, :])
        h = _layer_norm(x_ref[rows, :], lng_ref[...], lnb_ref[...])
        h2 = _layer_norm(DN_ALPHA * h + m, mg_ref[...], mb_ref[...])
        h2_ref[rows, :] = h2
        h2p_ref[rows, :] = _pack_rows(h2)

        h_hi = _bf16(h2)
        h_lo = _bf16(h2 - h_hi.astype(jnp.float32))
        part = _dot(h_hi, wr_cat)
        lrow = part + pltpu.roll(part, 128 - N_EXPERTS, axis=1) + _dot(h_lo, wr_hi)
        logits = lrow.T[0:N_EXPERTS, :] + br_ref[...]

        vals8 = jnp.zeros((8, sb), jnp.float32)
        idx8 = jnp.zeros((8, sb), jnp.int32)
        sels = []
        work = logits
        for kk in range(TOP_K):
            mval = jnp.max(work, axis=0, keepdims=True)
            midx = jnp.min(jnp.where(work == mval, e_iota, N_EXPERTS), axis=0, keepdims=True)
            sel = e_iota == midx
            sels.append(sel)
            vals8 = jnp.where(k_iota == kk, mval, vals8)
            idx8 = jnp.where(k_iota == kk, midx, idx8)
            work = jnp.where(sel, -jnp.inf, work)

        ex = jnp.where(k_iota < TOP_K, jnp.exp(vals8 - vals8[0:1, :]), 0.0)
        gates8 = ex / jnp.sum(ex, axis=0, keepdims=True)

        onehot = jnp.zeros((N_EXPERTS, sb), jnp.float32)
        for sel in sels:
            onehot = onehot + jnp.where(sel, 1.0, 0.0)
        cum = _dot(_bf16(onehot), tri[...])
        before = running + cum
        rank8 = jnp.zeros((8, sb), jnp.int32)
        for kk, sel in enumerate(sels):
            rk = jnp.sum(jnp.where(sel, before, 0.0), axis=0, keepdims=True).astype(jnp.int32)
            rank8 = jnp.where(k_iota == kk, rk, rank8)
        total = cum[:, sb - 1:sb] + onehot[:, sb - 1:sb]
        running = running + jnp.broadcast_to(total, (N_EXPERTS, sb))

        idx_ref[:, rows] = idx8
        rank_ref[:, rows] = rank8
        gpad = jnp.concatenate([gates8, jnp.zeros((128 - 8, sb), jnp.float32)], axis=0)
        gate_ref[rows, :] = gpad.T

    base[...] = running
    cnt_ref[...] = running[:, 0:128].astype(jnp.int32)


def _pool_inv_counts(seq_len):
    tm = TM_MIX
    assert seq_len // tm >= 3
    cols = []
    for w in POOL_WINDOWS:
        pos = np.arange(seq_len)
        lo = np.clip(pos - w // 2, 0, seq_len - 1)
        hi = np.clip(pos + (w - 1 - w // 2), 0, seq_len - 1)
        inv = (1.0 / (hi - lo + 1)).astype(np.float32)
        kinds = np.stack([inv[:tm], inv[tm:2 * tm], inv[seq_len - tm:]])
        cols.append(np.repeat(kinds[:, :, None], POOL_GROUP, axis=2))
    return jnp.asarray(np.concatenate(cols, axis=2))


def _mix_router(x2, ret, u, ln_g, ln_b, w_pool_bf, pool_scale, w_o_bf, mix_g, mix_b,
                wr_hi, wr_cat, br_full, seq_len):
    n = x2.shape[0]
    tm = TM_MIX
    sb = MIX_SUB
    halo = POOL_HALO
    hb = tm // halo
    n_halo_blocks = n // halo
    tiles_per_seq = seq_len // tm
    row = lambda i: (i, 0)
    const = lambda i: (0, 0)
    col = lambda i: (0, i)

    def inv_map(i):
        ts = i % tiles_per_seq
        return (jnp.where(ts == 0, 0, jnp.where(ts == tiles_per_seq - 1, 2, 1)), 0, 0)

    return pl.pallas_call(
        functools.partial(_mix_router_kernel, seq_len),
        grid=(n // tm,),
        in_specs=[
            pl.BlockSpec((tm, D_MODEL), row),
            pl.BlockSpec((tm, RET_WIDTH), row),
            pl.BlockSpec((tm, POOL_WIDTH), row),
            pl.BlockSpec((halo, POOL_WIDTH), lambda i: (jnp.maximum(i * hb - 1, 0), 0)),
            pl.BlockSpec((halo, POOL_WIDTH),
                         lambda i: (jnp.minimum((i + 1) * hb, n_halo_blocks - 1), 0)),
            pl.BlockSpec((1, tm, POOL_WIDTH), inv_map),
            pl.BlockSpec((1, D_MODEL), const),
            pl.BlockSpec((1, D_MODEL), const),
            pl.BlockSpec((POOL_WIDTH, POOL_WIDTH), const),
            pl.BlockSpec((1, POOL_WIDTH), const),
            pl.BlockSpec((D_MODEL, D_MODEL), const),
            pl.BlockSpec((1, D_MODEL), const),
            pl.BlockSpec((1, D_MODEL), const),
            pl.BlockSpec((D_MODEL, 128), const),
            pl.BlockSpec((D_MODEL, 128), const),
            pl.BlockSpec((N_EXPERTS, sb), const),
        ],
        out_specs=[
            pl.BlockSpec((tm, D_MODEL), row),
            pl.BlockSpec((tm, HALF), row),
            pl.BlockSpec((8, tm), col),
            pl.BlockSpec((8, tm), col),
            pl.BlockSpec((tm, 128), row),
            pl.BlockSpec((N_EXPERTS, 128), const),
        ],
        out_shape=[
            jax.ShapeDtypeStruct((n, D_MODEL), jnp.float32),
            jax.ShapeDtypeStruct((n, HALF), jnp.int32),
            jax.ShapeDtypeStruct((8, n), jnp.int32),
            jax.ShapeDtypeStruct((8, n), jnp.int32),
            jax.ShapeDtypeStruct((n, 128), jnp.float32),
            jax.ShapeDtypeStruct((N_EXPERTS, 128), jnp.int32),
        ],
        scratch_shapes=[
            pltpu.VMEM((sb, sb), jnp.bfloat16),
            pltpu.VMEM((N_EXPERTS, sb), jnp.float32),
        ],
        compiler_params=pltpu.CompilerParams(
            dimension_semantics=("arbitrary",), vmem_limit_bytes=VMEM_LIMIT),
        name="mix_router",
    )(x2, ret, u, u, u, _pool_inv_counts(seq_len), ln_g, ln_b, w_pool_bf, pool_scale, w_o_bf,
      mix_g, mix_b, wr_hi, wr_cat, br_full)


def _expert_kernel(it_tile, it_exp, it_rows, it_new, it_slot, it_next,
                   xs_ref, wg_hbm, bg_ref, wu_hbm, bu_ref, wd_hbm, bd_ref, ys_ref,
                   stage, wg_bf, wu_bf, wd_bf, sem):
    i = pl.program_id(0)
    n_valid = it_rows[i]
    w_hbm = (wg_hbm, wu_hbm, wd_hbm)
    w_bf = (wg_bf, wu_bf, wd_bf)

    def weight_copies(expert, slot):
        return [pltpu.make_async_copy(w_hbm[m].at[expert], stage.at[slot, m], sem.at[slot, m])
                for m in range(3)]

    @pl.when(i == 0)
    def _first_fetch():
        for c in weight_copies(it_exp[0], 0):
            c.start()

    @pl.when(it_new[i] == 1)
    def _switch_expert():
        slot = it_slot[i]
        for c in weight_copies(it_exp[i], slot):
            c.wait()
        nxt = it_next[i]

        @pl.when(nxt >= 0)
        def _():
            for c in weight_copies(nxt, 1 - slot):
                c.start()

        for m in range(3):
            w_bf[m][...] = _bf16(stage[slot, m])

    @pl.when(n_valid > 0)
    def _ffn():
        r = lax.broadcasted_iota(jnp.int32, (TM_EXPERT, HALF), 0)
        x_lo, x_hi = _unpack_rows(jnp.where(r < n_valid, xs_ref[...], 0))
        x_lo = _bf16(x_lo)
        x_hi = _bf16(x_hi)
        gt = _dot(x_lo, wg_bf[:HALF, :]) + _dot(x_hi, wg_bf[HALF:, :]) + bg_ref[0]
        up = _dot(x_lo, wu_bf[:HALF, :]) + _dot(x_hi, wu_bf[HALF:, :]) + bu_ref[0]
        gt = jnp.minimum(gt, SWIGLU_LIMIT)
        up = jnp.clip(up, -SWIGLU_LIMIT, SWIGLU_LIMIT)
        hid = (up + 1.0) * (gt * jax.nn.sigmoid(SWIGLU_ALPHA * gt))
        ys_ref[...] = _pack_rows(_dot(_bf16(hid), wd_bf[...]) + bd_ref[0])


def _experts(items, xs, w_gate, b_gate, w_up, b_up, w_down, b_down):
    it_tile, it_exp, it_rows, it_new, it_slot, it_next = items
    p = xs.shape[0]
    tm = TM_EXPERT
    n_items = it_tile.shape[0]
    tile_map = lambda i, t, e, *_: (t[i], 0)
    b_spec = pl.BlockSpec((1, 1, D_MODEL), lambda i, t, e, *_: (e[i], 0, 0))
    w_spec = pl.BlockSpec(memory_space=pl.ANY)
    w_scratch = pltpu.VMEM((D_MODEL, D_MODEL), jnp.bfloat16)
    grid_spec = pltpu.PrefetchScalarGridSpec(
        num_scalar_prefetch=6,
        grid=(n_items,),
        in_specs=[pl.BlockSpec((tm, HALF), tile_map),
                  w_spec, b_spec, w_spec, b_spec, w_spec, b_spec],
        out_specs=pl.BlockSpec((tm, HALF), tile_map),
        scratch_shapes=[pltpu.VMEM((2, 3, D_MODEL, D_MODEL), jnp.float32),
                        w_scratch, w_scratch, w_scratch,
                        pltpu.SemaphoreType.DMA((2, 3))],
    )
    return pl.pallas_call(
        _expert_kernel,
        grid_spec=grid_spec,
        out_shape=jax.ShapeDtypeStruct((p, HALF), jnp.int32),
        compiler_params=pltpu.CompilerParams(
            dimension_semantics=("arbitrary",), vmem_limit_bytes=VMEM_LIMIT),
        name="experts",
    )(it_tile, it_exp, it_rows, it_new, it_slot, it_next,
      xs, w_gate, b_gate, w_up, b_up, w_down, b_down)


def _padded_tiles(n_rows):
    return n_rows // TM_EXPERT + N_EXPERTS


def _expert_layout(counts, n_rows):
    tm = TM_EXPERT
    n_items = _padded_tiles(n_rows)
    tiles_e = (counts + tm - 1) // tm
    tile_end = jnp.cumsum(tiles_e)
    tile_start = tile_end - tiles_e
    total = tile_end[-1]
    i = jnp.arange(n_items, dtype=jnp.int32)
    tile = jnp.minimum(i, total - 1)
    e = jnp.sum((tile[:, None] >= tile_end[None, :]).astype(jnp.int32), axis=1)
    e = jnp.minimum(e, N_EXPERTS - 1)
    onehot = (e[:, None] == jnp.arange(N_EXPERTS, dtype=jnp.int32)[None, :]).astype(jnp.int32)
    pick = lambda a: jnp.sum(onehot * a[None, :], axis=1)
    first = pick(tile_start)
    valid = i < total
    rows = jnp.where(valid, jnp.clip(pick(counts) - (tile - first) * tm, 0, tm), 0)
    new = valid & (tile == first)
    slot = jnp.maximum(jnp.cumsum(new.astype(jnp.int32)) - 1, 0) % 2
    ids = jnp.arange(N_EXPERTS, dtype=jnp.int32)
    later = jnp.where((counts[None, :] > 0) & (ids[None, :] > ids[:, None]), ids[None, :], N_EXPERTS)
    next_e = jnp.min(later, axis=1)
    nxt = pick(jnp.where(next_e < N_EXPERTS, next_e, -1))
    items = tuple(a.astype(jnp.int32) for a in (tile, e, rows, new, slot, nxt))
    return tile_start * tm, items


def _combine_kernel(h2_ref, gate_ref, g_ref, b_ref, yk_ref, o_ref):
    gates = gate_ref[...]
    h2 = h2_ref[...]
    f_lo = jnp.zeros((h2.shape[0], HALF), jnp.float32)
    f_hi = jnp.zeros((h2.shape[0], HALF), jnp.float32)
    for kk in range(TOP_K):
        y_lo, y_hi = _unpack_rows(yk_ref[kk])
        gk = gates[:, kk:kk + 1]
        f_lo = f_lo + gk * y_lo
        f_hi = f_hi + gk * y_hi
    y = DN_ALPHA * h2 + jnp.concatenate([f_lo, f_hi], axis=1)
    o_ref[...] = _layer_norm(y, g_ref[...], b_ref[...])


def _combine(h2, gates, ffn_g, ffn_b, yk):
    n = h2.shape[0]
    tm = TM_COMBINE
    row = lambda i: (i, 0)
    const = lambda i: (0, 0)
    return pl.pallas_call(
        _combine_kernel,
        grid=(n // tm,),
        in_specs=[
            pl.BlockSpec((tm, D_MODEL), row),
            pl.BlockSpec((tm, 128), row),
            pl.BlockSpec((1, D_MODEL), const),
            pl.BlockSpec((1, D_MODEL), const),
            pl.BlockSpec((TOP_K, tm, HALF), lambda i: (0, i, 0)),
        ],
        out_specs=pl.BlockSpec((tm, D_MODEL), row),
        out_shape=jax.ShapeDtypeStruct((n, D_MODEL), jnp.float32),
        compiler_params=pltpu.CompilerParams(
            dimension_semantics=("arbitrary",), vmem_limit_bytes=VMEM_LIMIT),
        name="combine",
    )(h2, gates, ffn_g, ffn_b, yk)


SC_CORES = 2
SC_SUBCORES = 16
SC_CHUNK = 64


def _sc_dispatch(slots, h2, n_out_rows):
    n, w = h2.shape
    ch = SC_CHUNK
    n_chunks = n // (SC_CORES * SC_SUBCORES * ch)
    mesh = plsc.VectorSubcoreMesh(core_axis_name="c", subcore_axis_name="s")

    @functools.partial(
        pl.kernel, mesh=mesh,
        out_type=jax.ShapeDtypeStruct((n_out_rows, w), h2.dtype),
        scratch_types=[pltpu.VMEM((TOP_K, ch), jnp.int32), pltpu.VMEM((ch, w), h2.dtype)],
        name="sc_dispatch")
    def k(idx_hbm, h2_hbm, xs_hbm, idx_v, rows_v):
        wid = lax.axis_index("s") * SC_CORES + lax.axis_index("c")

        @pl.loop(0, n_chunks)
        def _(c):
            chunk = wid * n_chunks + c
            for kk in range(TOP_K):
                pltpu.sync_copy(idx_hbm.at[kk, pl.ds(chunk * ch, ch)], idx_v.at[kk])
            pltpu.sync_copy(h2_hbm.at[pl.ds(chunk * ch, ch)], rows_v)
            for kk in range(TOP_K):
                pltpu.sync_copy(rows_v, xs_hbm.at[idx_v.at[kk]])

    return k(slots, h2)


def _sc_gather(slots, ys):
    n = slots.shape[1]
    w = ys.shape[1]
    ch = SC_CHUNK
    n_chunks = n // (SC_CORES * SC_SUBCORES * ch)
    mesh = plsc.VectorSubcoreMesh(core_axis_name="c", subcore_axis_name="s")

    @functools.partial(
        pl.kernel, mesh=mesh,
        out_type=jax.ShapeDtypeStruct((TOP_K, n, w), ys.dtype),
        scratch_types=[pltpu.VMEM((TOP_K, ch), jnp.int32), pltpu.VMEM((ch, w), ys.dtype)],
        name="sc_gather")
    def k(idx_hbm, ys_hbm, out_hbm, idx_v, rows_v):
        wid = lax.axis_index("s") * SC_CORES + lax.axis_index("c")

        @pl.loop(0, n_chunks)
        def _(c):
            chunk = wid * n_chunks + c
            for kk in range(TOP_K):
                pltpu.sync_copy(idx_hbm.at[kk, pl.ds(chunk * ch, ch)], idx_v.at[kk])
            for kk in range(TOP_K):
                pltpu.sync_copy(ys_hbm.at[idx_v.at[kk]], rows_v)
                pltpu.sync_copy(rows_v, out_hbm.at[kk, pl.ds(chunk * ch, ch)])

    return k(slots, ys)


def _rotary_tables(seq_len):
    d = HEAD_DIM
    inv_freq = ROPE_BASE ** (-jnp.arange(0, d, 2, dtype=jnp.float32) / d)
    ang = jnp.arange(seq_len, dtype=jnp.float32)[:, None] * inv_freq[None, :]
    cos = jnp.cos(ang)
    sin = jnp.sin(ang)
    return jnp.concatenate([cos, cos], axis=1), jnp.concatenate([-sin, sin], axis=1)


def _trunk(x, p):
    bsz, seq_len, _ = x.shape
    n = bsz * seq_len
    x2 = x.reshape(n, D_MODEL)
    cos_t, sin_t = _rotary_tables(seq_len)

    q, k, v, sg, u = _ln_proj(x2, p["ln_in_g"], p["ln_in_b"], p["w_in"], cos_t, sin_t, seq_len)
    shp = (bsz, seq_len, RET_WIDTH)
    ret = _retention(q.reshape(shp), k.reshape(shp), v.reshape(shp), sg.reshape(shp),
                     p["dl_f"], p["dl_b"]).reshape(n, RET_WIDTH)

    h2, h2p, idx8, rank8, gates, cnt = _mix_router(
        x2, ret, u, p["ln_in_g"], p["ln_in_b"], p["w_pool"], p["pool_scale"], p["w_o"],
        p["ln_mix_g"], p["ln_mix_b"], p["wr_hi"], p["wr_cat"], p["br_full"], seq_len)

    starts, items = _expert_layout(cnt[:, 0], n * TOP_K)
    idx = idx8[:TOP_K]
    eq = idx[None, :, :] == jnp.arange(N_EXPERTS, dtype=jnp.int32)[:, None, None]
    slots = rank8[:TOP_K] + jnp.sum(jnp.where(eq, starts[:, None, None], 0), axis=0)

    xs = _sc_dispatch(slots, h2p, _padded_tiles(n * TOP_K) * TM_EXPERT)
    ys = _experts(items, xs, p["w_gate"], p["b_gate"], p["w_up"], p["b_up"],
                  p["w_down"], p["b_down"])
    yk = _sc_gather(slots, ys)
    out = _combine(h2, gates, p["ln_ffn_g"], p["ln_ffn_b"], yk)
    return out.reshape(bsz, seq_len, D_MODEL)


def kernel(x_prompt, x_sample, ln_in_g, ln_in_b, w_in, decay_logit_fwd, decay_logit_bwd, w_pool,
           pool_scale, w_o, ln_mix_g, ln_mix_b, w_router, b_router, w_gate, b_gate, w_up, b_up,
           w_down, b_down, ln_ffn_g, ln_ffn_b):
    row = lambda a: a.reshape(1, -1)
    wr = w_router[0]
    wr_hi = _bf16(wr)
    wr_lo = _bf16(wr - wr_hi.astype(jnp.float32))
    lane_pad = jnp.zeros((D_MODEL, 128 - 2 * N_EXPERTS), jnp.bfloat16)
    wr_cat = jnp.concatenate([wr_hi, wr_lo, lane_pad], axis=1)
    wr_hi = jnp.concatenate([wr_hi, jnp.zeros_like(wr_hi), lane_pad], axis=1)
    eye = jnp.eye(len(POOL_WINDOWS), dtype=jnp.float32)
    w_pool_bd = (eye[:, None, :, None] * w_pool[0][:, :, None, :]).reshape(POOL_WIDTH, POOL_WIDTH)
    p = {
        "ln_in_g": row(ln_in_g), "ln_in_b": row(ln_in_b),
        "w_in": _bf16(w_in[0]),
        "dl_f": decay_logit_fwd[0], "dl_b": decay_logit_bwd[0],
        "w_pool": _bf16(w_pool_bd), "pool_scale": row(pool_scale[0]),
        "w_o": _bf16(w_o[0]),
        "ln_mix_g": row(ln_mix_g[0]), "ln_mix_b": row(ln_mix_b[0]),
        "wr_hi": wr_hi, "wr_cat": wr_cat,
        "br_full": jnp.broadcast_to(b_router[0][:, None], (N_EXPERTS, MIX_SUB)),
        "w_gate": w_gate[0], "b_gate": b_gate[0][:, None, :],
        "w_up": w_up[0], "b_up": b_up[0][:, None, :],
        "w_down": w_down[0], "b_down": b_down[0][:, None, :],
        "ln_ffn_g": row(ln_ffn_g[0]), "ln_ffn_b": row(ln_ffn_b[0]),
    }
    return (_trunk(x_prompt, p), _trunk(x_sample, p))
```

```python
import functools

import jax
import jax.numpy as jnp
import numpy as np
from jax import lax
from jax.experimental import pallas as pl
from jax.experimental.pallas import tpu as pltpu
from jax.experimental.pallas import tpu_sc as plsc

D_MODEL = 1024
RET_WIDTH = 512
POOL_WIDTH = 512
N_HEADS = 4
HEAD_DIM = 128
ROPE_BASE = 10000.0
POOL_WINDOWS = (2, 4, 8, 16)
POOL_GROUP = 128
IN_WIDTH = 4 * RET_WIDTH + POOL_WIDTH
N_EXPERTS = 32
TOP_K = 4
SWIGLU_ALPHA = 1.702
SWIGLU_LIMIT = 7.0
LN_EPS = 1e-5
DN_ALPHA = 2.0 ** 0.25

TM_PROJ = 1024
RET_TILE = 1024
RET_CHUNK = 256
TM_MIX = 1024
MIX_SUB = 1024
POOL_HALO = 16
TM_EXPERT = 512
TM_COMBINE = 512

VMEM_LIMIT = 56 * 1024 * 1024

_NT = (((1,), (1,)), ((), ()))
_TN = (((0,), (0,)), ((), ()))


def _layer_norm(x, g, b):
    mu = jnp.mean(x, axis=-1, keepdims=True)
    xc = x - mu
    var = jnp.mean(xc * xc, axis=-1, keepdims=True)
    return xc * lax.rsqrt(var + LN_EPS) * g + b


def _bf16(x):
    return x.astype(jnp.bfloat16)


def _dot(a, b):
    return jnp.dot(a, b, preferred_element_type=jnp.float32)


HALF = D_MODEL // 2


def _pack_rows(x):
    lo = lax.bitcast_convert_type(_bf16(x[:, :HALF]).astype(jnp.float32), jnp.uint32)
    hi = lax.bitcast_convert_type(_bf16(x[:, HALF:]).astype(jnp.float32), jnp.uint32)
    return lax.bitcast_convert_type((lo >> 16) | hi, jnp.int32)


def _unpack_rows(w):
    u = lax.bitcast_convert_type(w, jnp.uint32)
    lo = lax.bitcast_convert_type(u << 16, jnp.float32)
    hi = lax.bitcast_convert_type(u & jnp.uint32(0xFFFF0000), jnp.float32)
    return lo, hi


def _ln_proj_kernel(x_ref, g_ref, b_ref, w_ref, cos_ref, sin_ref,
                    q_ref, k_ref, v_ref, sg_ref, u_ref):
    h = _bf16(_layer_norm(x_ref[...], g_ref[...], b_ref[...]))
    cos = cos_ref[...]
    sin = sin_ref[...]

    def rotary(t):
        return t * cos + pltpu.roll(t, HEAD_DIM // 2, axis=1) * sin

    R = RET_WIDTH
    pq = _dot(h, w_ref[:, 0:R])
    for hd in range(N_HEADS):
        sl = slice(hd * HEAD_DIM, (hd + 1) * HEAD_DIM)
        q_ref[:, sl] = _bf16(rotary(pq[:, sl]))
    pk = _dot(h, w_ref[:, R:2 * R])
    for hd in range(N_HEADS):
        sl = slice(hd * HEAD_DIM, (hd + 1) * HEAD_DIM)
        k_ref[:, sl] = _bf16(rotary(pk[:, sl]) * (HEAD_DIM ** -0.5))
    v_ref[...] = _bf16(_dot(h, w_ref[:, 2 * R:3 * R]))
    pg = _dot(h, w_ref[:, 3 * R:4 * R])
    sg_ref[...] = _bf16(pg * jax.nn.sigmoid(pg))
    u_ref[...] = _bf16(_dot(h, w_ref[:, 4 * R:]))


def _ln_proj(x2, ln_g, ln_b, w_in_bf, cos_t, sin_t, seq_len):
    n = x2.shape[0]
    tm = TM_PROJ
    tiles_per_seq = seq_len // tm
    row = lambda i: (i, 0)
    const = lambda i: (0, 0)
    out_sd = jax.ShapeDtypeStruct((n, RET_WIDTH), jnp.bfloat16)
    return pl.pallas_call(
        _ln_proj_kernel,
        grid=(n // tm,),
        in_specs=[
            pl.BlockSpec((tm, D_MODEL), row),
            pl.BlockSpec((1, D_MODEL), const),
            pl.BlockSpec((1, D_MODEL), const),
            pl.BlockSpec((D_MODEL, IN_WIDTH), const),
            pl.BlockSpec((tm, HEAD_DIM), lambda i: (i % tiles_per_seq, 0)),
            pl.BlockSpec((tm, HEAD_DIM), lambda i: (i % tiles_per_seq, 0)),
        ],
        out_specs=[pl.BlockSpec((tm, RET_WIDTH), row)] * 5,
        out_shape=[out_sd] * 5,
        compiler_params=pltpu.CompilerParams(
            dimension_semantics=("arbitrary",), vmem_limit_bytes=VMEM_LIMIT),
        name="ln_proj",
    )(x2, ln_g, ln_b, w_in_bf, cos_t, sin_t)


def _log_sigmoid(x):
    return jnp.minimum(x, 0.0) - jnp.log(1.0 + jnp.exp(-jnp.abs(x)))


def _retention_kernel(dlf_ref, dlb_ref, q_ref, k_ref, v_ref, sg_ref, o_ref,
                      dmat, xi_f, xi_b, zeta_f, zeta_b, cdec, s_f, s_b, snap):
    b = pl.program_id(0)
    phase = pl.program_id(1)
    j = pl.program_id(2)
    n_tiles = pl.num_programs(2)
    C = RET_CHUNK
    sub = RET_TILE // C

    @pl.when((b == 0) & (phase == 0) & (j == 0))
    def _init_tables():
        ri = lax.broadcasted_iota(jnp.int32, (C, C), 0).astype(jnp.float32)
        ci = lax.broadcasted_iota(jnp.int32, (C, C), 1).astype(jnp.float32)
        rel = ri - ci
        pos = lax.broadcasted_iota(jnp.int32, (C, HEAD_DIM), 0).astype(jnp.float32)
        zero_cc = jnp.zeros((C, C), jnp.float32)
        zero_cd = jnp.zeros((C, HEAD_DIM), jnp.float32)
        zero_dd = jnp.zeros((HEAD_DIM, HEAD_DIM), jnp.float32)
        for hd in range(N_HEADS):
            dlf = dlf_ref[hd]
            dlb = dlb_ref[hd]
            dmat[hd] = jnp.where(rel >= 0, jnp.exp(_log_sigmoid(zero_cc + dlf) * rel),
                                 jnp.exp(-_log_sigmoid(zero_cc + dlb) * rel))
            lf = _log_sigmoid(zero_cd + dlf)
            lb = _log_sigmoid(zero_cd + dlb)
            xi_f[hd] = jnp.exp(lf * (pos + 1.0))
            xi_b[hd] = jnp.exp(lb * (C - pos))
            zeta_f[hd] = jnp.exp(lf * (C - 1.0 - pos))
            zeta_b[hd] = jnp.exp(lb * pos)
            cdec[hd, 0] = jnp.exp(_log_sigmoid(zero_dd + dlf) * float(C))
            cdec[hd, 1] = jnp.exp(_log_sigmoid(zero_dd + dlb) * float(C))

    @pl.when(j == 0)
    def _reset_state():
        @pl.when(phase == 0)
        def _():
            s_b[...] = jnp.zeros_like(s_b)

        @pl.when(phase == 1)
        def _():
            s_f[...] = jnp.zeros_like(s_f)

    @pl.when(phase == 0)
    def _backward_states():
        tile = n_tiles - 1 - j

        def body(cc, carry):
            c = sub - 1 - cc
            r0 = pl.multiple_of(c * C, C)
            chunk = tile * sub + c
            for hd in range(N_HEADS):
                sl = slice(hd * HEAD_DIM, (hd + 1) * HEAD_DIM)
                kc = k_ref[0, pl.ds(r0, C), sl]
                vc = v_ref[0, pl.ds(r0, C), sl].astype(jnp.float32)
                st = s_b[hd]
                snap[chunk * N_HEADS + hd] = _bf16(st)
                kv = lax.dot_general(kc, _bf16(vc * zeta_b[hd]), _TN,
                                     preferred_element_type=jnp.float32)
                s_b[hd] = st * cdec[hd, 1] + kv
            return carry

        lax.fori_loop(0, sub, body, 0, unroll=True)

    @pl.when(phase == 1)
    def _forward_outputs():
        def body(c, carry):
            r0 = pl.multiple_of(c * C, C)
            chunk = j * sub + c
            for hd in range(N_HEADS):
                sl = slice(hd * HEAD_DIM, (hd + 1) * HEAD_DIM)
                qc = q_ref[0, pl.ds(r0, C), sl]
                kc = k_ref[0, pl.ds(r0, C), sl]
                vc = v_ref[0, pl.ds(r0, C), sl]
                scores = lax.dot_general(qc, kc, _NT, preferred_element_type=jnp.float32)
                o = _dot(_bf16(scores * dmat[hd]), vc)
                st = s_f[hd]
                both = jnp.concatenate([_bf16(st), snap[chunk * N_HEADS + hd]], axis=1)
                cross = _dot(qc, both)
                o = o + xi_f[hd] * cross[:, :HEAD_DIM] + xi_b[hd] * cross[:, HEAD_DIM:]
                kv = lax.dot_general(kc, _bf16(vc.astype(jnp.float32) * zeta_f[hd]), _TN,
                                     preferred_element_type=jnp.float32)
                s_f[hd] = st * cdec[hd, 0] + kv
                mu = jnp.mean(o, axis=-1, keepdims=True)
                oc = o - mu
                on = oc * lax.rsqrt(jnp.mean(oc * oc, axis=-1, keepdims=True) + LN_EPS)
                sg = sg_ref[0, pl.ds(r0, C), sl].astype(jnp.float32)
                o_ref[0, pl.ds(r0, C), sl] = _bf16(sg * on)
            return carry

        lax.fori_loop(0, sub, body, 0, unroll=True)


def _retention(q, k, v, sg, dl_f, dl_b):
    bsz, seq_len, _ = q.shape
    n_tiles = seq_len // RET_TILE
    n_chunks = seq_len // RET_CHUNK
    C = RET_CHUNK

    def kv_map(b, p, j):
        return (b, jnp.where(p == 0, n_tiles - 1 - j, j), 0)

    def q_map(b, p, j):
        return (b, jnp.where(p == 0, 0, j), 0)

    blk = (1, RET_TILE, RET_WIDTH)
    smem = pl.BlockSpec(memory_space=pltpu.SMEM)
    return pl.pallas_call(
        _retention_kernel,
        grid=(bsz, 2, n_tiles),
        in_specs=[smem, smem,
                  pl.BlockSpec(blk, q_map), pl.BlockSpec(blk, kv_map),
                  pl.BlockSpec(blk, kv_map), pl.BlockSpec(blk, q_map)],
        out_specs=pl.BlockSpec(blk, q_map),
        out_shape=jax.ShapeDtypeStruct(q.shape, jnp.bfloat16),
        scratch_shapes=[
            pltpu.VMEM((N_HEADS, C, C), jnp.float32),
            pltpu.VMEM((N_HEADS, C, HEAD_DIM), jnp.float32),
            pltpu.VMEM((N_HEADS, C, HEAD_DIM), jnp.float32),
            pltpu.VMEM((N_HEADS, C, HEAD_DIM), jnp.float32),
            pltpu.VMEM((N_HEADS, C, HEAD_DIM), jnp.float32),
            pltpu.VMEM((N_HEADS, 2, HEAD_DIM, HEAD_DIM), jnp.float32),
            pltpu.VMEM((N_HEADS, HEAD_DIM, HEAD_DIM), jnp.float32),
            pltpu.VMEM((N_HEADS, HEAD_DIM, HEAD_DIM), jnp.float32),
            pltpu.VMEM((n_chunks * N_HEADS, HEAD_DIM, HEAD_DIM), jnp.bfloat16),
        ],
        compiler_params=pltpu.CompilerParams(
            dimension_semantics=("arbitrary", "arbitrary", "arbitrary"),
            vmem_limit_bytes=VMEM_LIMIT),
        name="retention",
    )(dl_f, dl_b, q, k, v, sg)


def _mix_router_kernel(seq_len, x_ref, ret_ref, u_ref, up_ref, un_ref, inv_ref,
                       lng_ref, lnb_ref, wpool_ref, pscale_ref, wo_ref,
                       mg_ref, mb_ref, wr_hi_ref, wr_cat_ref, br_ref,
                       h2_ref, h2p_ref, idx_ref, rank_ref, gate_ref, cnt_ref,
                       tri, base):
    i = pl.program_id(0)
    tm = TM_MIX
    sb = MIX_SUB
    n_sub = tm // sb
    halo = POOL_HALO
    tiles_per_seq = seq_len // tm
    ts = i % tiles_per_seq

    @pl.when(i == 0)
    def _init():
        tr = lax.broadcasted_iota(jnp.int32, (sb, sb), 0)
        tc = lax.broadcasted_iota(jnp.int32, (sb, sb), 1)
        tri[...] = jnp.where(tr < tc, 1.0, 0.0).astype(jnp.bfloat16)
        base[...] = jnp.zeros_like(base)

    prev = jnp.where(ts == 0, jnp.zeros_like(up_ref[...]), up_ref[...])
    nxt = jnp.where(ts == tiles_per_seq - 1, jnp.zeros_like(un_ref[...]), un_ref[...])
    e_iota = lax.broadcasted_iota(jnp.int32, (N_EXPERTS, sb), 0)
    k_iota = lax.broadcasted_iota(jnp.int32, (8, sb), 0)
    wr_hi = wr_hi_ref[...]
    wr_cat = wr_cat_ref[...]
    running = base[...]

    for s in range(n_sub):
        r0 = s * sb
        rows = slice(r0, r0 + sb)
        u_cur = u_ref[rows, :]
        above = prev if s == 0 else u_ref[r0 - halo:r0, :]
        below = nxt if s == n_sub - 1 else u_ref[r0 + sb:r0 + sb + halo, :]
        u_ext = jnp.concatenate([above, u_cur, below], axis=0).astype(jnp.float32)
        n_ext = sb + 2 * halo

        sums = []
        for gi, w in enumerate(POOL_WINDOWS):
            a = u_ext[:, gi * POOL_GROUP:(gi + 1) * POOL_GROUP]
            span = 1
            while span < w // 2:
                a = a + pltpu.roll(a, n_ext - span, axis=0)
                span *= 2
            sums.append(a + pltpu.roll(a, w // 2, axis=0))
        wsum = jnp.concatenate(sums, axis=1)[halo:halo + sb, :]
        pooled = wsum * inv_ref[0, rows, :] - u_cur.astype(jnp.float32)
        pool = _bf16(_dot(_bf16(pooled), wpool_ref[...]) * pscale_ref[...])

        m = _dot(ret_ref[rows, :], wo_ref[0:RET_WIDTH, :]) + _dot(pool, wo_ref[RET_WIDTH:, :])
        h = _layer_norm(x_ref[rows, :], lng_ref[...], lnb_ref[...])
        h2 = _layer_norm(DN_ALPHA * h + m, mg_ref[...], mb_ref[...])
        h2_ref[rows, :] = h2
        h2p_ref[rows, :] = _pack_rows(h2)

        h_hi = _bf16(h2)
        h_lo = _bf16(h2 - h_hi.astype(jnp.float32))
        part = _dot(h_hi, wr_cat)
        lrow = part + pltpu.roll(part, 128 - N_EXPERTS, axis=1) + _dot(h_lo, wr_hi)
        logits = lrow.T[0:N_EXPERTS, :] + br_ref[...]

        vals8 = jnp.zeros((8, sb), jnp.float32)
        idx8 = jnp.zeros((8, sb), jnp.int32)
        sels = []
        work = logits
        for kk in range(TOP_K):
            mval = jnp.max(work, axis=0, keepdims=True)
            midx = jnp.min(jnp.where(work == mval, e_iota, N_EXPERTS), axis=0, keepdims=True)
            sel = e_iota == midx
            sels.append(sel)
            vals8 = jnp.where(k_iota == kk, mval, vals8)
            idx8 = jnp.where(k_iota == kk, midx, idx8)
            work = jnp.where(sel, -jnp.inf, work)

        ex = jnp.where(k_iota < TOP_K, jnp.exp(vals8 - vals8[0:1, :]), 0.0)
        gates8 = ex / jnp.sum(ex, axis=0, keepdims=True)

        onehot = jnp.zeros((N_EXPERTS, sb), jnp.float32)
        for sel in sels:
            onehot = onehot + jnp.where(sel, 1.0, 0.0)
        cum = _dot(_bf16(onehot), tri[...])
        before = running + cum
        rank8 = jnp.zeros((8, sb), jnp.int32)
        for kk, sel in enumerate(sels):
            rk = jnp.sum(jnp.where(sel, before, 0.0), axis=0, keepdims=True).astype(jnp.int32)
            rank8 = jnp.where(k_iota == kk, rk, rank8)
        total = cum[:, sb - 1:sb] + onehot[:, sb - 1:sb]
        running = running + jnp.broadcast_to(total, (N_EXPERTS, sb))

        idx_ref[:, rows] = idx8
        rank_ref[:, rows] = rank8
        gpad = jnp.concatenate([gates8, jnp.zeros((128 - 8, sb), jnp.float32)], axis=0)
        gate_ref[rows, :] = gpad.T

    base[...] = running
    cnt_ref[...] = running[:, 0:128].astype(jnp.int32)


def _pool_inv_counts(seq_len):
    tm = TM_MIX
    assert seq_len // tm >= 3
    cols = []
    for w in POOL_WINDOWS:
        pos = np.arange(seq_len)
        lo = np.clip(pos - w // 2, 0, seq_len - 1)
        hi = np.clip(pos + (w - 1 - w // 2), 0, seq_len - 1)
        inv = (1.0 / (hi - lo + 1)).astype(np.float32)
        kinds = np.stack([inv[:tm], inv[tm:2 * tm], inv[seq_len - tm:]])
        cols.append(np.repeat(kinds[:, :, None], POOL_GROUP, axis=2))
    return jnp.asarray(np.concatenate(cols, axis=2))


def _mix_router(x2, ret, u, ln_g, ln_b, w_pool_bf, pool_scale, w_o_bf, mix_g, mix_b,
                wr_hi, wr_cat, br_full, seq_len):
    n = x2.shape[0]
    tm = TM_MIX
    sb = MIX_SUB
    halo = POOL_HALO
    hb = tm // halo
    n_halo_blocks = n // halo
    tiles_per_seq = seq_len // tm
    row = lambda i: (i, 0)
    const = lambda i: (0, 0)
    col = lambda i: (0, i)

    def inv_map(i):
        ts = i % tiles_per_seq
        return (jnp.where(ts == 0, 0, jnp.where(ts == tiles_per_seq - 1, 2, 1)), 0, 0)

    return pl.pallas_call(
        functools.partial(_mix_router_kernel, seq_len),
        grid=(n // tm,),
        in_specs=[
            pl.BlockSpec((tm, D_MODEL), row),
            pl.BlockSpec((tm, RET_WIDTH), row),
            pl.BlockSpec((tm, POOL_WIDTH), row),
            pl.BlockSpec((halo, POOL_WIDTH), lambda i: (jnp.maximum(i * hb - 1, 0), 0)),
            pl.BlockSpec((halo, POOL_WIDTH),
                         lambda i: (jnp.minimum((i + 1) * hb, n_halo_blocks - 1), 0)),
            pl.BlockSpec((1, tm, POOL_WIDTH), inv_map),
            pl.BlockSpec((1, D_MODEL), const),
            pl.BlockSpec((1, D_MODEL), const),
            pl.BlockSpec((POOL_WIDTH, POOL_WIDTH), const),
            pl.BlockSpec((1, POOL_WIDTH), const),
            pl.BlockSpec((D_MODEL, D_MODEL), const),
            pl.BlockSpec((1, D_MODEL), const),
            pl.BlockSpec((1, D_MODEL), const),
            pl.BlockSpec((D_MODEL, 128), const),
            pl.BlockSpec((D_MODEL, 128), const),
            pl.BlockSpec((N_EXPERTS, sb), const),
        ],
        out_specs=[
            pl.BlockSpec((tm, D_MODEL), row),
            pl.BlockSpec((tm, HALF), row),
            pl.BlockSpec((8, tm), col),
            pl.BlockSpec((8, tm), col),
            pl.BlockSpec((tm, 128), row),
            pl.BlockSpec((N_EXPERTS, 128), const),
        ],
        out_shape=[
            jax.ShapeDtypeStruct((n, D_MODEL), jnp.float32),
            jax.ShapeDtypeStruct((n, HALF), jnp.int32),
            jax.ShapeDtypeStruct((8, n), jnp.int32),
            jax.ShapeDtypeStruct((8, n), jnp.int32),
            jax.ShapeDtypeStruct((n, 128), jnp.float32),
            jax.ShapeDtypeStruct((N_EXPERTS, 128), jnp.int32),
        ],
        scratch_shapes=[
            pltpu.VMEM((sb, sb), jnp.bfloat16),
            pltpu.VMEM((N_EXPERTS, sb), jnp.float32),
        ],
        compiler_params=pltpu.CompilerParams(
            dimension_semantics=("arbitrary",), vmem_limit_bytes=VMEM_LIMIT),
        name="mix_router",
    )(x2, ret, u, u, u, _pool_inv_counts(seq_len), ln_g, ln_b, w_pool_bf, pool_scale, w_o_bf,
      mix_g, mix_b, wr_hi, wr_cat, br_full)


def _expert_kernel(it_tile, it_exp, it_rows, it_new, it_slot, it_next,
                   xs_ref, wg_hbm, bg_ref, wu_hbm, bu_ref, wd_hbm, bd_ref, ys_ref,
                   stage, wg_bf, wu_bf, wd_bf, sem):
    i = pl.program_id(0)
    n_valid = it_rows[i]
    w_hbm = (wg_hbm, wu_hbm, wd_hbm)
    w_bf = (wg_bf, wu_bf, wd_bf)

    def weight_copies(expert, slot):
        return [pltpu.make_async_copy(w_hbm[m].at[expert], stage.at[slot, m], sem.at[slot, m])
                for m in range(3)]

    @pl.when(i == 0)
    def _first_fetch():
        for c in weight_copies(it_exp[0], 0):
            c.start()

    @pl.when(it_new[i] == 1)
    def _switch_expert():
        slot = it_slot[i]
        for c in weight_copies(it_exp[i], slot):
            c.wait()
        nxt = it_next[i]

        @pl.when(nxt >= 0)
        def _():
            for c in weight_copies(nxt, 1 - slot):
                c.start()

        for m in range(3):
            w_bf[m][...] = _bf16(stage[slot, m])

    @pl.when(n_valid > 0)
    def _ffn():
        r = lax.broadcasted_iota(jnp.int32, (TM_EXPERT, HALF), 0)
        x_lo, x_hi = _unpack_rows(jnp.where(r < n_valid, xs_ref[...], 0))
        x = jnp.concatenate([_bf16(x_lo), _bf16(x_hi)], axis=1)
        gt = _dot(x, wg_bf[...]) + bg_ref[0]
        up = _dot(x, wu_bf[...]) + bu_ref[0]
        gt = jnp.minimum(gt, SWIGLU_LIMIT)
        up = jnp.clip(up, -SWIGLU_LIMIT, SWIGLU_LIMIT)
        hid = (up + 1.0) * (gt * jax.nn.sigmoid(SWIGLU_ALPHA * gt))
        ys_ref[...] = _pack_rows(_dot(_bf16(hid), wd_bf[...]) + bd_ref[0])


def _experts(items, xs, w_gate, b_gate, w_up, b_up, w_down, b_down):
    it_tile, it_exp, it_rows, it_new, it_slot, it_next = items
    p = xs.shape[0]
    tm = TM_EXPERT
    n_items = it_tile.shape[0]
    tile_map = lambda i, t, e, *_: (t[i], 0)
    b_spec = pl.BlockSpec((1, 1, D_MODEL), lambda i, t, e, *_: (e[i], 0, 0))
    w_spec = pl.BlockSpec(memory_space=pl.ANY)
    w_scratch = pltpu.VMEM((D_MODEL, D_MODEL), jnp.bfloat16)
    grid_spec = pltpu.PrefetchScalarGridSpec(
        num_scalar_prefetch=6,
        grid=(n_items,),
        in_specs=[pl.BlockSpec((tm, HALF), tile_map),
                  w_spec, b_spec, w_spec, b_spec, w_spec, b_spec],
        out_specs=pl.BlockSpec((tm, HALF), tile_map),
        scratch_shapes=[pltpu.VMEM((2, 3, D_MODEL, D_MODEL), jnp.float32),
                        w_scratch, w_scratch, w_scratch,
                        pltpu.SemaphoreType.DMA((2, 3))],
    )
    return pl.pallas_call(
        _expert_kernel,
        grid_spec=grid_spec,
        out_shape=jax.ShapeDtypeStruct((p, HALF), jnp.int32),
        compiler_params=pltpu.CompilerParams(
            dimension_semantics=("arbitrary",), vmem_limit_bytes=VMEM_LIMIT),
        name="experts",
    )(it_tile, it_exp, it_rows, it_new, it_slot, it_next,
      xs, w_gate, b_gate, w_up, b_up, w_down, b_down)


def _padded_tiles(n_rows):
    return n_rows // TM_EXPERT + N_EXPERTS


def _expert_layout(counts, n_rows):
    tm = TM_EXPERT
    n_items = _padded_tiles(n_rows)
    tiles_e = (counts + tm - 1) // tm
    tile_end = jnp.cumsum(tiles_e)
    tile_start = tile_end - tiles_e
    total = tile_end[-1]
    i = jnp.arange(n_items, dtype=jnp.int32)
    tile = jnp.minimum(i, total - 1)
    e = jnp.sum((tile[:, None] >= tile_end[None, :]).astype(jnp.int32), axis=1)
    e = jnp.minimum(e, N_EXPERTS - 1)
    onehot = (e[:, None] == jnp.arange(N_EXPERTS, dtype=jnp.int32)[None, :]).astype(jnp.int32)
    pick = lambda a: jnp.sum(onehot * a[None, :], axis=1)
    first = pick(tile_start)
    valid = i < total
    rows = jnp.where(valid, jnp.clip(pick(counts) - (tile - first) * tm, 0, tm), 0)
    new = valid & (tile == first)
    slot = jnp.maximum(jnp.cumsum(new.astype(jnp.int32)) - 1, 0) % 2
    ids = jnp.arange(N_EXPERTS, dtype=jnp.int32)
    later = jnp.where((counts[None, :] > 0) & (ids[None, :] > ids[:, None]), ids[None, :], N_EXPERTS)
    next_e = jnp.min(later, axis=1)
    nxt = pick(jnp.where(next_e < N_EXPERTS, next_e, -1))
    items = tuple(a.astype(jnp.int32) for a in (tile, e, rows, new, slot, nxt))
    return tile_start * tm, items


def _combine_kernel(h2_ref, gate_ref, g_ref, b_ref, yk_ref, o_ref):
    gates = gate_ref[...]
    h2 = h2_ref[...]
    f_lo = jnp.zeros((h2.shape[0], HALF), jnp.float32)
    f_hi = jnp.zeros((h2.shape[0], HALF), jnp.float32)
    for kk in range(TOP_K):
        y_lo, y_hi = _unpack_rows(yk_ref[kk])
        gk = gates[:, kk:kk + 1]
        f_lo = f_lo + gk * y_lo
        f_hi = f_hi + gk * y_hi
    y = DN_ALPHA * h2 + jnp.concatenate([f_lo, f_hi], axis=1)
    o_ref[...] = _layer_norm(y, g_ref[...], b_ref[...])


def _combine(h2, gates, ffn_g, ffn_b, yk):
    n = h2.shape[0]
    tm = TM_COMBINE
    row = lambda i: (i, 0)
    const = lambda i: (0, 0)
    return pl.pallas_call(
        _combine_kernel,
        grid=(n // tm,),
        in_specs=[
            pl.BlockSpec((tm, D_MODEL), row),
            pl.BlockSpec((tm, 128), row),
            pl.BlockSpec((1, D_MODEL), const),
            pl.BlockSpec((1, D_MODEL), const),
            pl.BlockSpec((TOP_K, tm, HALF), lambda i: (0, i, 0)),
        ],
        out_specs=pl.BlockSpec((tm, D_MODEL), row),
        out_shape=jax.ShapeDtypeStruct((n, D_MODEL), jnp.float32),
        compiler_params=pltpu.CompilerParams(
            dimension_semantics=("arbitrary",), vmem_limit_bytes=VMEM_LIMIT),
        name="combine",
    )(h2, gates, ffn_g, ffn_b, yk)


SC_CORES = 2
SC_SUBCORES = 16
SC_CHUNK = 64


def _sc_dispatch(slots, h2, n_out_rows):
    n, w = h2.shape
    ch = SC_CHUNK
    n_chunks = n // (SC_CORES * SC_SUBCORES * ch)
    mesh = plsc.VectorSubcoreMesh(core_axis_name="c", subcore_axis_name="s")

    @functools.partial(
        pl.kernel, mesh=mesh,
        out_type=jax.ShapeDtypeStruct((n_out_rows, w), h2.dtype),
        scratch_types=[pltpu.VMEM((TOP_K, ch), jnp.int32), pltpu.VMEM((ch, w), h2.dtype)],
        name="sc_dispatch")
    def k(idx_hbm, h2_hbm, xs_hbm, idx_v, rows_v):
        wid = lax.axis_index("s") * SC_CORES + lax.axis_index("c")

        @pl.loop(0, n_chunks)
        def _(c):
            chunk = wid * n_chunks + c
            for kk in range(TOP_K):
                pltpu.sync_copy(idx_hbm.at[kk, pl.ds(chunk * ch, ch)], idx_v.at[kk])
            pltpu.sync_copy(h2_hbm.at[pl.ds(chunk * ch, ch)], rows_v)
            for kk in range(TOP_K):
                pltpu.sync_copy(rows_v, xs_hbm.at[idx_v.at[kk]])

    return k(slots, h2)


def _sc_gather(slots, ys):
    n = slots.shape[1]
    w = ys.shape[1]
    ch = SC_CHUNK
    n_chunks = n // (SC_CORES * SC_SUBCORES * ch)
    mesh = plsc.VectorSubcoreMesh(core_axis_name="c", subcore_axis_name="s")

    @functools.partial(
        pl.kernel, mesh=mesh,
        out_type=jax.ShapeDtypeStruct((TOP_K, n, w), ys.dtype),
        scratch_types=[pltpu.VMEM((TOP_K, ch), jnp.int32), pltpu.VMEM((ch, w), ys.dtype)],
        name="sc_gather")
    def k(idx_hbm, ys_hbm, out_hbm, idx_v, rows_v):
        wid = lax.axis_index("s") * SC_CORES + lax.axis_index("c")

        @pl.loop(0, n_chunks)
        def _(c):
            chunk = wid * n_chunks + c
            for kk in range(TOP_K):
                pltpu.sync_copy(idx_hbm.at[kk, pl.ds(chunk * ch, ch)], idx_v.at[kk])
            for kk in range(TOP_K):
                pltpu.sync_copy(ys_hbm.at[idx_v.at[kk]], rows_v)
                pltpu.sync_copy(rows_v, out_hbm.at[kk, pl.ds(chunk * ch, ch)])

    return k(slots, ys)


def _rotary_tables(seq_len):
    d = HEAD_DIM
    inv_freq = np.float32(ROPE_BASE) ** (-np.arange(0, d, 2, dtype=np.float32) / np.float32(d))
    ang = np.arange(seq_len, dtype=np.float32)[:, None] * inv_freq[None, :].astype(np.float32)
    cos = np.cos(ang).astype(np.float32)
    sin = np.sin(ang).astype(np.float32)
    return (jnp.asarray(np.concatenate([cos, cos], axis=1)),
            jnp.asarray(np.concatenate([-sin, sin], axis=1)))


def _trunk(x, p):
    bsz, seq_len, _ = x.shape
    n = bsz * seq_len
    x2 = x.reshape(n, D_MODEL)
    cos_t, sin_t = _rotary_tables(seq_len)

    q, k, v, sg, u = _ln_proj(x2, p["ln_in_g"], p["ln_in_b"], p["w_in"], cos_t, sin_t, seq_len)
    shp = (bsz, seq_len, RET_WIDTH)
    ret = _retention(q.reshape(shp), k.reshape(shp), v.reshape(shp), sg.reshape(shp),
                     p["dl_f"], p["dl_b"]).reshape(n, RET_WIDTH)

    h2, h2p, idx8, rank8, gates, cnt = _mix_router(
        x2, ret, u, p["ln_in_g"], p["ln_in_b"], p["w_pool"], p["pool_scale"], p["w_o"],
        p["ln_mix_g"], p["ln_mix_b"], p["wr_hi"], p["wr_cat"], p["br_full"], seq_len)

    starts, items = _expert_layout(cnt[:, 0], n * TOP_K)
    idx = idx8[:TOP_K]
    eq = idx[None, :, :] == jnp.arange(N_EXPERTS, dtype=jnp.int32)[:, None, None]
    slots = rank8[:TOP_K] + jnp.sum(jnp.where(eq, starts[:, None, None], 0), axis=0)

    xs = _sc_dispatch(slots, h2p, _padded_tiles(n * TOP_K) * TM_EXPERT)
    ys = _experts(items, xs, p["w_gate"], p["b_gate"], p["w_up"], p["b_up"],
                  p["w_down"], p["b_down"])
    yk = _sc_gather(slots, ys)
    out = _combine(h2, gates, p["ln_ffn_g"], p["ln_ffn_b"], yk)
    return out.reshape(bsz, seq_len, D_MODEL)


def kernel(x_prompt, x_sample, ln_in_g, ln_in_b, w_in, decay_logit_fwd, decay_logit_bwd, w_pool,
           pool_scale, w_o, ln_mix_g, ln_mix_b, w_router, b_router, w_gate, b_gate, w_up, b_up,
           w_down, b_down, ln_ffn_g, ln_ffn_b):
    row = lambda a: a.reshape(1, -1)
    wr = w_router[0]
    wr_hi = _bf16(wr)
    wr_lo = _bf16(wr - wr_hi.astype(jnp.float32))
    lane_pad = jnp.zeros((D_MODEL, 128 - 2 * N_EXPERTS), jnp.bfloat16)
    wr_cat = jnp.concatenate([wr_hi, wr_lo, lane_pad], axis=1)
    wr_hi = jnp.concatenate([wr_hi, jnp.zeros_like(wr_hi), lane_pad], axis=1)
    eye = jnp.eye(len(POOL_WINDOWS), dtype=jnp.float32)
    w_pool_bd = (eye[:, None, :, None] * w_pool[0][:, :, None, :]).reshape(POOL_WIDTH, POOL_WIDTH)
    p = {
        "ln_in_g": row(ln_in_g), "ln_in_b": row(ln_in_b),
        "w_in": _bf16(w_in[0]),
        "dl_f": decay_logit_fwd[0], "dl_b": decay_logit_bwd[0],
        "w_pool": _bf16(w_pool_bd), "pool_scale": row(pool_scale[0]),
        "w_o": _bf16(w_o[0]),
        "ln_mix_g": row(ln_mix_g[0]), "ln_mix_b": row(ln_mix_b[0]),
        "wr_hi": wr_hi, "wr_cat": wr_cat,
        "br_full": jnp.broadcast_to(b_router[0][:, None], (N_EXPERTS, MIX_SUB)),
        "w_gate": w_gate[0], "b_gate": b_gate[0][:, None, :],
        "w_up": w_up[0], "b_up": b_up[0][:, None, :],
        "w_down": w_down[0], "b_down": b_down[0][:, None, :],
        "ln_ffn_g": row(ln_ffn_g[0]), "ln_ffn_b": row(ln_ffn_b[0]),
    }
    return (_trunk(x_prompt, p), _trunk(x_sample, p))
```

```python
import functools

import jax
import jax.numpy as jnp
import numpy as np
from jax import lax
from jax.experimental import pallas as pl
from jax.experimental.pallas import tpu as pltpu
from jax.experimental.pallas import tpu_sc as plsc

D_MODEL = 1024
RET_WIDTH = 512
POOL_WIDTH = 512
N_HEADS = 4
HEAD_DIM = 128
ROPE_BASE = 10000.0
POOL_WINDOWS = (2, 4, 8, 16)
POOL_GROUP = 128
IN_WIDTH = 4 * RET_WIDTH + POOL_WIDTH
N_EXPERTS = 32
TOP_K = 4
SWIGLU_ALPHA = 1.702
SWIGLU_LIMIT = 7.0
LN_EPS = 1e-5
DN_ALPHA = 2.0 ** 0.25

TM_PROJ = 1024
RET_TILE = 2048
RET_CHUNK = 256
TM_MIX = 1024
MIX_SUB = 1024
POOL_HALO = 16
TM_EXPERT = 512
TM_COMBINE = 512

VMEM_LIMIT = 56 * 1024 * 1024

_NT = (((1,), (1,)), ((), ()))
_TN = (((0,), (0,)), ((), ()))


def _ln_stats(x):
    mu = jnp.mean(x, axis=-1, keepdims=True)
    xc = x - mu
    var = jnp.mean(xc * xc, axis=-1, keepdims=True)
    return mu, lax.rsqrt(var + LN_EPS)


def _layer_norm(x, g, b):
    mu, rstd = _ln_stats(x)
    return (x - mu) * rstd * g + b


def _bf16(x):
    return x.astype(jnp.bfloat16)


def _dot(a, b):
    return jnp.dot(a, b, preferred_element_type=jnp.float32)


HALF = D_MODEL // 2


def _pack_rows(x):
    return _pack_rounded(_bf16(x).astype(jnp.float32))


def _pack_rounded(xr):
    lo = lax.bitcast_convert_type(xr[:, :HALF], jnp.uint32)
    hi = lax.bitcast_convert_type(xr[:, HALF:], jnp.uint32)
    return lax.bitcast_convert_type((lo >> 16) | hi, jnp.int32)


def _unpack_rows(w):
    u = lax.bitcast_convert_type(w, jnp.uint32)
    lo = lax.bitcast_convert_type(u << 16, jnp.float32)
    hi = lax.bitcast_convert_type(u & jnp.uint32(0xFFFF0000), jnp.float32)
    return lo, hi


def _ln_proj_kernel(x_ref, g_ref, b_ref, w_ref, cos_ref, sin_ref,
                    q_ref, k_ref, v_ref, sg_ref, u_ref, st_ref):
    x = x_ref[...]
    mu, rstd = _ln_stats(x)
    lane = lax.broadcasted_iota(jnp.int32, st_ref.shape, 1)
    st_ref[...] = jnp.where(lane == 0, mu, jnp.where(lane == 1, rstd, 0.0))
    h = _bf16((x - mu) * rstd * g_ref[...] + b_ref[...])
    cos = cos_ref[...]
    sin = sin_ref[...]

    def rotary(t):
        return t * cos + pltpu.roll(t, HEAD_DIM // 2, axis=1) * sin

    R = RET_WIDTH
    pq = _dot(h, w_ref[:, 0:R])
    for hd in range(N_HEADS):
        sl = slice(hd * HEAD_DIM, (hd + 1) * HEAD_DIM)
        q_ref[:, sl] = _bf16(rotary(pq[:, sl]))
    pk = _dot(h, w_ref[:, R:2 * R])
    for hd in range(N_HEADS):
        sl = slice(hd * HEAD_DIM, (hd + 1) * HEAD_DIM)
        k_ref[:, sl] = _bf16(rotary(pk[:, sl]) * (HEAD_DIM ** -0.5))
    v_ref[...] = _bf16(_dot(h, w_ref[:, 2 * R:3 * R]))
    pg = _dot(h, w_ref[:, 3 * R:4 * R])
    sg_ref[...] = _bf16(pg * jax.nn.sigmoid(pg))
    u_ref[...] = _bf16(_dot(h, w_ref[:, 4 * R:]))


def _ln_proj(x2, ln_g, ln_b, w_in_bf, cos_t, sin_t, seq_len):
    n = x2.shape[0]
    tm = TM_PROJ
    tiles_per_seq = seq_len // tm
    row = lambda i: (i, 0)
    const = lambda i: (0, 0)
    out_sd = jax.ShapeDtypeStruct((n, RET_WIDTH), jnp.bfloat16)
    return pl.pallas_call(
        _ln_proj_kernel,
        grid=(n // tm,),
        in_specs=[
            pl.BlockSpec((tm, D_MODEL), row),
            pl.BlockSpec((1, D_MODEL), const),
            pl.BlockSpec((1, D_MODEL), const),
            pl.BlockSpec((D_MODEL, IN_WIDTH), const),
            pl.BlockSpec((tm, HEAD_DIM), lambda i: (i % tiles_per_seq, 0)),
            pl.BlockSpec((tm, HEAD_DIM), lambda i: (i % tiles_per_seq, 0)),
        ],
        out_specs=[pl.BlockSpec((tm, RET_WIDTH), row)] * 5 + [pl.BlockSpec((tm, 128), row)],
        out_shape=[out_sd] * 5 + [jax.ShapeDtypeStruct((n, 128), jnp.float32)],
        compiler_params=pltpu.CompilerParams(
            dimension_semantics=("arbitrary",), vmem_limit_bytes=VMEM_LIMIT),
        name="ln_proj",
    )(x2, ln_g, ln_b, w_in_bf, cos_t, sin_t)


def _log_sigmoid(x):
    return jnp.minimum(x, 0.0) - jnp.log(1.0 + jnp.exp(-jnp.abs(x)))


def _retention_kernel(dlf_ref, dlb_ref, q_ref, k_ref, v_ref, sg_ref, o_ref,
                      dmat, xi_f, xi_b, zeta_f, zeta_b, cdec, s_f, s_b, snap):
    b = pl.program_id(0)
    phase = pl.program_id(1)
    j = pl.program_id(2)
    n_tiles = pl.num_programs(2)
    C = RET_CHUNK
    sub = RET_TILE // C

    @pl.when((b == 0) & (phase == 0) & (j == 0))
    def _init_tables():
        ri = lax.broadcasted_iota(jnp.int32, (C, C), 0).astype(jnp.float32)
        ci = lax.broadcasted_iota(jnp.int32, (C, C), 1).astype(jnp.float32)
        rel = ri - ci
        pos = lax.broadcasted_iota(jnp.int32, (C, HEAD_DIM), 0).astype(jnp.float32)
        zero_cc = jnp.zeros((C, C), jnp.float32)
        zero_cd = jnp.zeros((C, HEAD_DIM), jnp.float32)
        zero_dd = jnp.zeros((HEAD_DIM, HEAD_DIM), jnp.float32)
        for hd in range(N_HEADS):
            dlf = dlf_ref[hd]
            dlb = dlb_ref[hd]
            dmat[hd] = jnp.where(rel >= 0, jnp.exp(_log_sigmoid(zero_cc + dlf) * rel),
                                 jnp.exp(-_log_sigmoid(zero_cc + dlb) * rel))
            lf = _log_sigmoid(zero_cd + dlf)
            lb = _log_sigmoid(zero_cd + dlb)
            xi_f[hd] = jnp.exp(lf * (pos + 1.0))
            xi_b[hd] = jnp.exp(lb * (C - pos))
            zeta_f[hd] = jnp.exp(lf * (C - 1.0 - pos))
            zeta_b[hd] = jnp.exp(lb * pos)
            cdec[hd, 0] = jnp.exp(_log_sigmoid(zero_dd + dlf) * float(C))
            cdec[hd, 1] = jnp.exp(_log_sigmoid(zero_dd + dlb) * float(C))

    @pl.when(j == 0)
    def _reset_state():
        @pl.when(phase == 0)
        def _():
            s_b[...] = jnp.zeros_like(s_b)

        @pl.when(phase == 1)
        def _():
            s_f[...] = jnp.zeros_like(s_f)

    @pl.when(phase == 0)
    def _backward_states():
        tile = n_tiles - 1 - j

        def body(cc, carry):
            c = sub - 1 - cc
            r0 = pl.multiple_of(c * C, C)
            chunk = tile * sub + c
            for hd in range(N_HEADS):
                sl = slice(hd * HEAD_DIM, (hd + 1) * HEAD_DIM)
                kc = k_ref[0, pl.ds(r0, C), sl]
                vc = v_ref[0, pl.ds(r0, C), sl].astype(jnp.float32)
                st = s_b[hd]
                snap[chunk * N_HEADS + hd] = _bf16(st)
                kv = lax.dot_general(kc, _bf16(vc * zeta_b[hd]), _TN,
                                     preferred_element_type=jnp.float32)
                s_b[hd] = st * cdec[hd, 1] + kv
            return carry

        lax.fori_loop(0, sub, body, 0, unroll=True)

    @pl.when(phase == 1)
    def _forward_outputs():
        def body(c, carry):
            r0 = pl.multiple_of(c * C, C)
            chunk = j * sub + c
            for hd in range(N_HEADS):
                sl = slice(hd * HEAD_DIM, (hd + 1) * HEAD_DIM)
                qc = q_ref[0, pl.ds(r0, C), sl]
                kc = k_ref[0, pl.ds(r0, C), sl]
                vc = v_ref[0, pl.ds(r0, C), sl]
                scores = lax.dot_general(qc, kc, _NT, preferred_element_type=jnp.float32)
                o = _dot(_bf16(scores * dmat[hd]), vc)
                st = s_f[hd]
                both = jnp.concatenate([_bf16(st), snap[chunk * N_HEADS + hd]], axis=1)
                cross = _dot(qc, both)
                o = o + xi_f[hd] * cross[:, :HEAD_DIM] + xi_b[hd] * cross[:, HEAD_DIM:]
                kv = lax.dot_general(kc, _bf16(vc.astype(jnp.float32) * zeta_f[hd]), _TN,
                                     preferred_element_type=jnp.float32)
                s_f[hd] = st * cdec[hd, 0] + kv
                mu = jnp.mean(o, axis=-1, keepdims=True)
                oc = o - mu
                on = oc * lax.rsqrt(jnp.mean(oc * oc, axis=-1, keepdims=True) + LN_EPS)
                sg = sg_ref[0, pl.ds(r0, C), sl].astype(jnp.float32)
                o_ref[0, pl.ds(r0, C), sl] = _bf16(sg * on)
            return carry

        lax.fori_loop(0, sub, body, 0, unroll=True)


def _retention(q, k, v, sg, dl_f, dl_b):
    bsz, seq_len, _ = q.shape
    n_tiles = seq_len // RET_TILE
    n_chunks = seq_len // RET_CHUNK
    C = RET_CHUNK

    def kv_map(b, p, j):
        return (b, jnp.where(p == 0, n_tiles - 1 - j, j), 0)

    def q_map(b, p, j):
        return (b, jnp.where(p == 0, 0, j), 0)

    blk = (1, RET_TILE, RET_WIDTH)
    smem = pl.BlockSpec(memory_space=pltpu.SMEM)
    return pl.pallas_call(
        _retention_kernel,
        grid=(bsz, 2, n_tiles),
        in_specs=[smem, smem,
                  pl.BlockSpec(blk, q_map), pl.BlockSpec(blk, kv_map),
                  pl.BlockSpec(blk, kv_map), pl.BlockSpec(blk, q_map)],
        out_specs=pl.BlockSpec(blk, q_map),
        out_shape=jax.ShapeDtypeStruct(q.shape, jnp.bfloat16),
        scratch_shapes=[
            pltpu.VMEM((N_HEADS, C, C), jnp.float32),
            pltpu.VMEM((N_HEADS, C, HEAD_DIM), jnp.float32),
            pltpu.VMEM((N_HEADS, C, HEAD_DIM), jnp.float32),
            pltpu.VMEM((N_HEADS, C, HEAD_DIM), jnp.float32),
            pltpu.VMEM((N_HEADS, C, HEAD_DIM), jnp.float32),
            pltpu.VMEM((N_HEADS, 2, HEAD_DIM, HEAD_DIM), jnp.float32),
            pltpu.VMEM((N_HEADS, HEAD_DIM, HEAD_DIM), jnp.float32),
            pltpu.VMEM((N_HEADS, HEAD_DIM, HEAD_DIM), jnp.float32),
            pltpu.VMEM((n_chunks * N_HEADS, HEAD_DIM, HEAD_DIM), jnp.bfloat16),
        ],
        compiler_params=pltpu.CompilerParams(
            dimension_semantics=("arbitrary", "arbitrary", "arbitrary"),
            vmem_limit_bytes=VMEM_LIMIT),
        name="retention",
    )(dl_f, dl_b, q, k, v, sg)


def _mix_router_kernel(seq_len, x_ref, st_ref, ret_ref, u_ref, up_ref, un_ref, inv_ref,
                       lng_ref, lnb_ref, wpool_ref, pscale_ref, wo_ref,
                       mg_ref, mb_ref, wr_hi_ref, wr_cat_ref, br_ref,
                       h2_ref, h2p_ref, idx_ref, rank_ref, gate_ref, cnt_ref,
                       tri, base):
    i = pl.program_id(0)
    tm = TM_MIX
    sb = MIX_SUB
    n_sub = tm // sb
    halo = POOL_HALO
    tiles_per_seq = seq_len // tm
    ts = i % tiles_per_seq

    @pl.when(i == 0)
    def _init():
        tr = lax.broadcasted_iota(jnp.int32, (sb, sb), 0)
        tc = lax.broadcasted_iota(jnp.int32, (sb, sb), 1)
        tri[...] = jnp.where(tr < tc, 1.0, 0.0).astype(jnp.bfloat16)
        base[...] = jnp.zeros_like(base)

    prev = jnp.where(ts == 0, jnp.zeros_like(up_ref[...]), up_ref[...])
    nxt = jnp.where(ts == tiles_per_seq - 1, jnp.zeros_like(un_ref[...]), un_ref[...])
    e_iota = lax.broadcasted_iota(jnp.int32, (N_EXPERTS, sb), 0)
    k_iota = lax.broadcasted_iota(jnp.int32, (8, sb), 0)
    wr_hi = wr_hi_ref[...]
    wr_cat = wr_cat_ref[...]
    running = base[...]

    for s in range(n_sub):
        r0 = s * sb
        rows = slice(r0, r0 + sb)
        u_cur = u_ref[rows, :]
        above = prev if s == 0 else u_ref[r0 - halo:r0, :]
        below = nxt if s == n_sub - 1 else u_ref[r0 + sb:r0 + sb + halo, :]
        u_ext = jnp.concatenate([above, u_cur, below], axis=0).astype(jnp.float32)
        n_ext = sb + 2 * halo

        sums = []
        for gi, w in enumerate(POOL_WINDOWS):
            a = u_ext[:, gi * POOL_GROUP:(gi + 1) * POOL_GROUP]
            span = 1
            while span < w // 2:
                a = a + pltpu.roll(a, n_ext - span, axis=0)
                span *= 2
            sums.append(a + pltpu.roll(a, w // 2, axis=0))
        wsum = jnp.concatenate(sums, axis=1)[halo:halo + sb, :]
        pooled = wsum * inv_ref[0, rows, :] - u_cur.astype(jnp.float32)
        pool = _bf16(_dot(_bf16(pooled), wpool_ref[...]) * pscale_ref[...])

        m = _dot(ret_ref[rows, :], wo_ref[0:RET_WIDTH, :]) + _dot(pool, wo_ref[RET_WIDTH:, :])
        st = st_ref[rows, :]
        h = (x_ref[rows, :] - st[:, 0:1]) * st[:, 1:2] * lng_ref[...] + lnb_ref[...]
        h2 = _layer_norm(DN_ALPHA * h + m, mg_ref[...], mb_ref[...])
        h2_ref[rows, :] = h2
        h_hi = _bf16(h2)
        h_hi32 = h_hi.astype(jnp.float32)
        h2p_ref[rows, :] = _pack_rounded(h_hi32)

        h_lo = _bf16(h2 - h_hi32)
        part = _dot(h_hi, wr_cat)
        lrow = part + pltpu.roll(part, 128 - N_EXPERTS, axis=1) + _dot(h_lo, wr_hi)
        logits = lrow.T[0:N_EXPERTS, :] + br_ref[...]

        vals8 = jnp.zeros((8, sb), jnp.float32)
        idx8 = jnp.zeros((8, sb), jnp.int32)
        sels = []
        work = logits
        for kk in range(TOP_K):
            mval = jnp.max(work, axis=0, keepdims=True)
            midx = jnp.min(jnp.where(work == mval, e_iota, N_EXPERTS), axis=0, keepdims=True)
            sel = e_iota == midx
            sels.append(sel)
            vals8 = jnp.where(k_iota == kk, mval, vals8)
            idx8 = jnp.where(k_iota == kk, midx, idx8)
            work = jnp.where(sel, -jnp.inf, work)

        ex = jnp.where(k_iota < TOP_K, jnp.exp(vals8 - vals8[0:1, :]), 0.0)
        gates8 = ex / jnp.sum(ex, axis=0, keepdims=True)

        onehot = jnp.zeros((N_EXPERTS, sb), jnp.float32)
        for sel in sels:
            onehot = onehot + jnp.where(sel, 1.0, 0.0)
        cum = _dot(_bf16(onehot), tri[...])
        before = running + cum
        rank8 = jnp.zeros((8, sb), jnp.int32)
        for kk, sel in enumerate(sels):
            rk = jnp.sum(jnp.where(sel, before, 0.0), axis=0, keepdims=True).astype(jnp.int32)
            rank8 = jnp.where(k_iota == kk, rk, rank8)
        total = cum[:, sb - 1:sb] + onehot[:, sb - 1:sb]
        running = running + jnp.broadcast_to(total, (N_EXPERTS, sb))

        idx_ref[:, rows] = idx8
        rank_ref[:, rows] = rank8
        gpad = jnp.concatenate([gates8, jnp.zeros((128 - 8, sb), jnp.float32)], axis=0)
        gate_ref[rows, :] = gpad.T

    base[...] = running
    cnt_ref[...] = running[:, 0:128].astype(jnp.int32)


def _pool_inv_counts(seq_len):
    tm = TM_MIX
    assert seq_len // tm >= 3
    cols = []
    for w in POOL_WINDOWS:
        pos = np.arange(seq_len)
        lo = np.clip(pos - w // 2, 0, seq_len - 1)
        hi = np.clip(pos + (w - 1 - w // 2), 0, seq_len - 1)
        inv = (1.0 / (hi - lo + 1)).astype(np.float32)
        kinds = np.stack([inv[:tm], inv[tm:2 * tm], inv[seq_len - tm:]])
        cols.append(np.repeat(kinds[:, :, None], POOL_GROUP, axis=2))
    return jnp.asarray(np.concatenate(cols, axis=2))


def _mix_router(x2, st, ret, u, ln_g, ln_b, w_pool_bf, pool_scale, w_o_bf, mix_g, mix_b,
                wr_hi, wr_cat, br_full, seq_len):
    n = x2.shape[0]
    tm = TM_MIX
    sb = MIX_SUB
    halo = POOL_HALO
    hb = tm // halo
    n_halo_blocks = n // halo
    tiles_per_seq = seq_len // tm
    row = lambda i: (i, 0)
    const = lambda i: (0, 0)
    col = lambda i: (0, i)

    def inv_map(i):
        ts = i % tiles_per_seq
        return (jnp.where(ts == 0, 0, jnp.where(ts == tiles_per_seq - 1, 2, 1)), 0, 0)

    return pl.pallas_call(
        functools.partial(_mix_router_kernel, seq_len),
        grid=(n // tm,),
        in_specs=[
            pl.BlockSpec((tm, D_MODEL), row),
            pl.BlockSpec((tm, 128), row),
            pl.BlockSpec((tm, RET_WIDTH), row),
            pl.BlockSpec((tm, POOL_WIDTH), row),
            pl.BlockSpec((halo, POOL_WIDTH), lambda i: (jnp.maximum(i * hb - 1, 0), 0)),
            pl.BlockSpec((halo, POOL_WIDTH),
                         lambda i: (jnp.minimum((i + 1) * hb, n_halo_blocks - 1), 0)),
            pl.BlockSpec((1, tm, POOL_WIDTH), inv_map),
            pl.BlockSpec((1, D_MODEL), const),
            pl.BlockSpec((1, D_MODEL), const),
            pl.BlockSpec((POOL_WIDTH, POOL_WIDTH), const),
            pl.BlockSpec((1, POOL_WIDTH), const),
            pl.BlockSpec((D_MODEL, D_MODEL), const),
            pl.BlockSpec((1, D_MODEL), const),
            pl.BlockSpec((1, D_MODEL), const),
            pl.BlockSpec((D_MODEL, 128), const),
            pl.BlockSpec((D_MODEL, 128), const),
            pl.BlockSpec((N_EXPERTS, sb), const),
        ],
        out_specs=[
            pl.BlockSpec((tm, D_MODEL), row),
            pl.BlockSpec((tm, HALF), row),
            pl.BlockSpec((8, tm), col),
            pl.BlockSpec((8, tm), col),
            pl.BlockSpec((tm, 128), row),
            pl.BlockSpec((N_EXPERTS, 128), const),
        ],
        out_shape=[
            jax.ShapeDtypeStruct((n, D_MODEL), jnp.float32),
            jax.ShapeDtypeStruct((n, HALF), jnp.int32),
            jax.ShapeDtypeStruct((8, n), jnp.int32),
            jax.ShapeDtypeStruct((8, n), jnp.int32),
            jax.ShapeDtypeStruct((n, 128), jnp.float32),
            jax.ShapeDtypeStruct((N_EXPERTS, 128), jnp.int32),
        ],
        scratch_shapes=[
            pltpu.VMEM((sb, sb), jnp.bfloat16),
            pltpu.VMEM((N_EXPERTS, sb), jnp.float32),
        ],
        compiler_params=pltpu.CompilerParams(
            dimension_semantics=("arbitrary",), vmem_limit_bytes=VMEM_LIMIT),
        name="mix_router",
    )(x2, st, ret, u, u, u, _pool_inv_counts(seq_len), ln_g, ln_b, w_pool_bf, pool_scale, w_o_bf,
      mix_g, mix_b, wr_hi, wr_cat, br_full)


def _expert_kernel(it_tile, it_exp, it_rows, it_new, it_slot, it_next,
                   xs_ref, wg_hbm, bg_ref, wu_hbm, bu_ref, wd_hbm, bd_ref, ys_ref,
                   stage, wg_bf, wu_bf, wd_bf, sem):
    i = pl.program_id(0)
    n_valid = it_rows[i]
    w_hbm = (wg_hbm, wu_hbm, wd_hbm)
    w_bf = (wg_bf, wu_bf, wd_bf)

    def weight_copies(expert, slot):
        return [pltpu.make_async_copy(w_hbm[m].at[expert], stage.at[slot, m], sem.at[slot, m])
                for m in range(3)]

    @pl.when(i == 0)
    def _first_fetch():
        for c in weight_copies(it_exp[0], 0):
            c.start()

    @pl.when(it_new[i] == 1)
    def _switch_expert():
        slot = it_slot[i]
        for c in weight_copies(it_exp[i], slot):
            c.wait()
        nxt = it_next[i]

        @pl.when(nxt >= 0)
        def _():
            for c in weight_copies(nxt, 1 - slot):
                c.start()

        for m in range(3):
            w_bf[m][...] = _bf16(stage[slot, m])

    @pl.when(n_valid > 0)
    def _ffn():
        r = lax.broadcasted_iota(jnp.int32, (TM_EXPERT, HALF), 0)
        x_lo, x_hi = _unpack_rows(jnp.where(r < n_valid, xs_ref[...], 0))
        x = jnp.concatenate([_bf16(x_lo), _bf16(x_hi)], axis=1)
        gt = _dot(x, wg_bf[...]) + bg_ref[0]
        up = _dot(x, wu_bf[...]) + bu_ref[0]
        gt = jnp.minimum(gt, SWIGLU_LIMIT)
        up = jnp.clip(up, -SWIGLU_LIMIT, SWIGLU_LIMIT)
        hid = (up + 1.0) * (gt * jax.nn.sigmoid(SWIGLU_ALPHA * gt))
        ys_ref[...] = _pack_rows(_dot(_bf16(hid), wd_bf[...]) + bd_ref[0])


def _experts(items, xs, w_gate, b_gate, w_up, b_up, w_down, b_down):
    it_tile, it_exp, it_rows, it_new, it_slot, it_next = items
    p = xs.shape[0]
    tm = TM_EXPERT
    n_items = it_tile.shape[0]
    tile_map = lambda i, t, e, *_: (t[i], 0)
    b_spec = pl.BlockSpec((1, 1, D_MODEL), lambda i, t, e, *_: (e[i], 0, 0))
    w_spec = pl.BlockSpec(memory_space=pl.ANY)
    w_scratch = pltpu.VMEM((D_MODEL, D_MODEL), jnp.bfloat16)
    grid_spec = pltpu.PrefetchScalarGridSpec(
        num_scalar_prefetch=6,
        grid=(n_items,),
        in_specs=[pl.BlockSpec((tm, HALF), tile_map),
                  w_spec, b_spec, w_spec, b_spec, w_spec, b_spec],
        out_specs=pl.BlockSpec((tm, HALF), tile_map),
        scratch_shapes=[pltpu.VMEM((2, 3, D_MODEL, D_MODEL), jnp.float32),
                        w_scratch, w_scratch, w_scratch,
                        pltpu.SemaphoreType.DMA((2, 3))],
    )
    return pl.pallas_call(
        _expert_kernel,
        grid_spec=grid_spec,
        out_shape=jax.ShapeDtypeStruct((p, HALF), jnp.int32),
        compiler_params=pltpu.CompilerParams(
            dimension_semantics=("arbitrary",), vmem_limit_bytes=VMEM_LIMIT),
        name="experts",
    )(it_tile, it_exp, it_rows, it_new, it_slot, it_next,
      xs, w_gate, b_gate, w_up, b_up, w_down, b_down)


def _padded_tiles(n_rows):
    return n_rows // TM_EXPERT + N_EXPERTS


def _expert_layout(counts, n_rows):
    tm = TM_EXPERT
    n_items = _padded_tiles(n_rows)
    tiles_e = (counts + tm - 1) // tm
    tile_end = jnp.cumsum(tiles_e)
    tile_start = tile_end - tiles_e
    total = tile_end[-1]
    i = jnp.arange(n_items, dtype=jnp.int32)
    tile = jnp.minimum(i, total - 1)
    e = jnp.sum((tile[:, None] >= tile_end[None, :]).astype(jnp.int32), axis=1)
    e = jnp.minimum(e, N_EXPERTS - 1)
    onehot = (e[:, None] == jnp.arange(N_EXPERTS, dtype=jnp.int32)[None, :]).astype(jnp.int32)
    pick = lambda a: jnp.sum(onehot * a[None, :], axis=1)
    first = pick(tile_start)
    valid = i < total
    rows = jnp.where(valid, jnp.clip(pick(counts) - (tile - first) * tm, 0, tm), 0)
    new = valid & (tile == first)
    slot = jnp.maximum(jnp.cumsum(new.astype(jnp.int32)) - 1, 0) % 2
    ids = jnp.arange(N_EXPERTS, dtype=jnp.int32)
    later = jnp.where((counts[None, :] > 0) & (ids[None, :] > ids[:, None]), ids[None, :], N_EXPERTS)
    next_e = jnp.min(later, axis=1)
    nxt = pick(jnp.where(next_e < N_EXPERTS, next_e, -1))
    items = tuple(a.astype(jnp.int32) for a in (tile, e, rows, new, slot, nxt))
    return tile_start * tm, items


def _combine_kernel(h2_ref, gate_ref, g_ref, b_ref, yk_ref, o_ref):
    gates = gate_ref[...]
    h2 = h2_ref[...]
    f_lo = jnp.zeros((h2.shape[0], HALF), jnp.float32)
    f_hi = jnp.zeros((h2.shape[0], HALF), jnp.float32)
    for kk in range(TOP_K):
        y_lo, y_hi = _unpack_rows(yk_ref[kk])
        gk = gates[:, kk:kk + 1]
        f_lo = f_lo + gk * y_lo
        f_hi = f_hi + gk * y_hi
    y = DN_ALPHA * h2 + jnp.concatenate([f_lo, f_hi], axis=1)
    o_ref[...] = _layer_norm(y, g_ref[...], b_ref[...])


def _combine(h2, gates, ffn_g, ffn_b, yk):
    n = h2.shape[0]
    tm = TM_COMBINE
    row = lambda i: (i, 0)
    const = lambda i: (0, 0)
    return pl.pallas_call(
        _combine_kernel,
        grid=(n // tm,),
        in_specs=[
            pl.BlockSpec((tm, D_MODEL), row),
            pl.BlockSpec((tm, 128), row),
            pl.BlockSpec((1, D_MODEL), const),
            pl.BlockSpec((1, D_MODEL), const),
            pl.BlockSpec((TOP_K, tm, HALF), lambda i: (0, i, 0)),
        ],
        out_specs=pl.BlockSpec((tm, D_MODEL), row),
        out_shape=jax.ShapeDtypeStruct((n, D_MODEL), jnp.float32),
        compiler_params=pltpu.CompilerParams(
            dimension_semantics=("arbitrary",), vmem_limit_bytes=VMEM_LIMIT),
        name="combine",
    )(h2, gates, ffn_g, ffn_b, yk)


SC_CORES = 2
SC_SUBCORES = 16
SC_CHUNK = 64


def _sc_dispatch(slots, h2, n_out_rows):
    n, w = h2.shape
    ch = SC_CHUNK
    n_chunks = n // (SC_CORES * SC_SUBCORES * ch)
    mesh = plsc.VectorSubcoreMesh(core_axis_name="c", subcore_axis_name="s")

    @functools.partial(
        pl.kernel, mesh=mesh,
        out_type=jax.ShapeDtypeStruct((n_out_rows, w), h2.dtype),
        scratch_types=[pltpu.VMEM((TOP_K, ch), jnp.int32), pltpu.VMEM((ch, w), h2.dtype)],
        name="sc_dispatch")
    def k(idx_hbm, h2_hbm, xs_hbm, idx_v, rows_v):
        wid = lax.axis_index("s") * SC_CORES + lax.axis_index("c")

        @pl.loop(0, n_chunks)
        def _(c):
            chunk = wid * n_chunks + c
            for kk in range(TOP_K):
                pltpu.sync_copy(idx_hbm.at[kk, pl.ds(chunk * ch, ch)], idx_v.at[kk])
            pltpu.sync_copy(h2_hbm.at[pl.ds(chunk * ch, ch)], rows_v)
            for kk in range(TOP_K):
                pltpu.sync_copy(rows_v, xs_hbm.at[idx_v.at[kk]])

    return k(slots, h2)


def _sc_gather(slots, ys):
    n = slots.shape[1]
    w = ys.shape[1]
    ch = SC_CHUNK
    n_chunks = n // (SC_CORES * SC_SUBCORES * ch)
    mesh = plsc.VectorSubcoreMesh(core_axis_name="c", subcore_axis_name="s")

    @functools.partial(
        pl.kernel, mesh=mesh,
        out_type=jax.ShapeDtypeStruct((TOP_K, n, w), ys.dtype),
        scratch_types=[pltpu.VMEM((TOP_K, ch), jnp.int32), pltpu.VMEM((ch, w), ys.dtype)],
        name="sc_gather")
    def k(idx_hbm, ys_hbm, out_hbm, idx_v, rows_v):
        wid = lax.axis_index("s") * SC_CORES + lax.axis_index("c")

        @pl.loop(0, n_chunks)
        def _(c):
            chunk = wid * n_chunks + c
            for kk in range(TOP_K):
                pltpu.sync_copy(idx_hbm.at[kk, pl.ds(chunk * ch, ch)], idx_v.at[kk])
            for kk in range(TOP_K):
                pltpu.sync_copy(ys_hbm.at[idx_v.at[kk]], rows_v)
                pltpu.sync_copy(rows_v, out_hbm.at[kk, pl.ds(chunk * ch, ch)])

    return k(slots, ys)


def _rotary_tables(seq_len):
    d = HEAD_DIM
    inv_freq = np.float32(ROPE_BASE) ** (-np.arange(0, d, 2, dtype=np.float32) / np.float32(d))
    ang = np.arange(seq_len, dtype=np.float32)[:, None] * inv_freq[None, :].astype(np.float32)
    cos = np.cos(ang).astype(np.float32)
    sin = np.sin(ang).astype(np.float32)
    return (jnp.asarray(np.concatenate([cos, cos], axis=1)),
            jnp.asarray(np.concatenate([-sin, sin], axis=1)))


def _trunk(x, p):
    bsz, seq_len, _ = x.shape
    n = bsz * seq_len
    x2 = x.reshape(n, D_MODEL)
    cos_t, sin_t = _rotary_tables(seq_len)

    q, k, v, sg, u, st = _ln_proj(x2, p["ln_in_g"], p["ln_in_b"], p["w_in"], cos_t, sin_t, seq_len)
    shp = (bsz, seq_len, RET_WIDTH)
    ret = _retention(q.reshape(shp), k.reshape(shp), v.reshape(shp), sg.reshape(shp),
                     p["dl_f"], p["dl_b"]).reshape(n, RET_WIDTH)

    h2, h2p, idx8, rank8, gates, cnt = _mix_router(
        x2, st, ret, u, p["ln_in_g"], p["ln_in_b"], p["w_pool"], p["pool_scale"], p["w_o"],
        p["ln_mix_g"], p["ln_mix_b"], p["wr_hi"], p["wr_cat"], p["br_full"], seq_len)

    starts, items = _expert_layout(cnt[:, 0], n * TOP_K)
    idx = idx8[:TOP_K]
    eq = idx[None, :, :] == jnp.arange(N_EXPERTS, dtype=jnp.int32)[:, None, None]
    slots = rank8[:TOP_K] + jnp.sum(jnp.where(eq, starts[:, None, None], 0), axis=0)

    xs = _sc_dispatch(slots, h2p, _padded_tiles(n * TOP_K) * TM_EXPERT)
    ys = _experts(items, xs, p["w_gate"], p["b_gate"], p["w_up"], p["b_up"],
                  p["w_down"], p["b_down"])
    yk = _sc_gather(slots, ys)
    out = _combine(h2, gates, p["ln_ffn_g"], p["ln_ffn_b"], yk)
    return out.reshape(bsz, seq_len, D_MODEL)


def kernel(x_prompt, x_sample, ln_in_g, ln_in_b, w_in, decay_logit_fwd, decay_logit_bwd, w_pool,
           pool_scale, w_o, ln_mix_g, ln_mix_b, w_router, b_router, w_gate, b_gate, w_up, b_up,
           w_down, b_down, ln_ffn_g, ln_ffn_b):
    row = lambda a: a.reshape(1, -1)
    wr = w_router[0]
    wr_hi = _bf16(wr)
    wr_lo = _bf16(wr - wr_hi.astype(jnp.float32))
    lane_pad = jnp.zeros((D_MODEL, 128 - 2 * N_EXPERTS), jnp.bfloat16)
    wr_cat = jnp.concatenate([wr_hi, wr_lo, lane_pad], axis=1)
    wr_hi = jnp.concatenate([wr_hi, jnp.zeros_like(wr_hi), lane_pad], axis=1)
    eye = jnp.eye(len(POOL_WINDOWS), dtype=jnp.float32)
    w_pool_bd = (eye[:, None, :, None] * w_pool[0][:, :, None, :]).reshape(POOL_WIDTH, POOL_WIDTH)
    p = {
        "ln_in_g": row(ln_in_g), "ln_in_b": row(ln_in_b),
        "w_in": _bf16(w_in[0]),
        "dl_f": decay_logit_fwd[0], "dl_b": decay_logit_bwd[0],
        "w_pool": _bf16(w_pool_bd), "pool_scale": row(pool_scale[0]),
        "w_o": _bf16(w_o[0]),
        "ln_mix_g": row(ln_mix_g[0]), "ln_mix_b": row(ln_mix_b[0]),
        "wr_hi": wr_hi, "wr_cat": wr_cat,
        "br_full": jnp.broadcast_to(b_router[0][:, None], (N_EXPERTS, MIX_SUB)),
        "w_gate": w_gate[0], "b_gate": b_gate[0][:, None, :],
        "w_up": w_up[0], "b_up": b_up[0][:, None, :],
        "w_down": w_down[0], "b_down": b_down[0][:, None, :],
        "ln_ffn_g": row(ln_ffn_g[0]), "ln_ffn_b": row(ln_ffn_b[0]),
    }
    return (_trunk(x_prompt, p), _trunk(x_sample, p))
```

```python
import dataclasses
import functools

import jax
import jax.numpy as jnp
import numpy as np
from jax import lax
from jax.experimental import pallas as pl
from jax.experimental.pallas import tpu as pltpu
from jax.experimental.pallas import tpu_sc as plsc

D_MODEL = 1024
RET_WIDTH = 512
POOL_WIDTH = 512
N_HEADS = 4
HEAD_DIM = 128
ROPE_BASE = 10000.0
POOL_WINDOWS = (2, 4, 8, 16)
POOL_GROUP = 128
IN_WIDTH = 4 * RET_WIDTH + POOL_WIDTH
N_EXPERTS = 32
TOP_K = 4
SWIGLU_ALPHA = 1.702
SWIGLU_LIMIT = 7.0
LN_EPS = 1e-5
DN_ALPHA = 2.0 ** 0.25

TM_PROJ = 1024
RET_TILE = 2048
RET_CHUNK = 256
TM_MIX = 1024
MIX_SUB = 1024
POOL_HALO = 16
TM_EXPERT = 512
TM_COMBINE = 512

VMEM_LIMIT = 56 * 1024 * 1024

_NT = (((1,), (1,)), ((), ()))
_TN = (((0,), (0,)), ((), ()))


def _ln_stats(x):
    mu = jnp.mean(x, axis=-1, keepdims=True)
    xc = x - mu
    var = jnp.mean(xc * xc, axis=-1, keepdims=True)
    return mu, lax.rsqrt(var + LN_EPS)


def _layer_norm(x, g, b):
    mu, rstd = _ln_stats(x)
    return (x - mu) * rstd * g + b


def _bf16(x):
    return x.astype(jnp.bfloat16)


def _dot(a, b):
    return jnp.dot(a, b, preferred_element_type=jnp.float32)


HALF = D_MODEL // 2


def _pack_rows(x):
    return _pack_rounded(_bf16(x).astype(jnp.float32))


def _pack_rounded(xr):
    lo = lax.bitcast_convert_type(xr[:, :HALF], jnp.uint32)
    hi = lax.bitcast_convert_type(xr[:, HALF:], jnp.uint32)
    return lax.bitcast_convert_type((lo >> 16) | hi, jnp.int32)


def _unpack_rows(w):
    u = lax.bitcast_convert_type(w, jnp.uint32)
    lo = lax.bitcast_convert_type(u << 16, jnp.float32)
    hi = lax.bitcast_convert_type(u & jnp.uint32(0xFFFF0000), jnp.float32)
    return lo, hi


def _ln_proj_kernel(x_ref, g_ref, b_ref, w_ref, cos_ref, sin_ref,
                    q_ref, k_ref, v_ref, sg_ref, u_ref, st_ref):
    x = x_ref[...]
    mu, rstd = _ln_stats(x)
    lane = lax.broadcasted_iota(jnp.int32, st_ref.shape, 1)
    st_ref[...] = jnp.where(lane == 0, mu, jnp.where(lane == 1, rstd, 0.0))
    h = _bf16((x - mu) * rstd * g_ref[...] + b_ref[...])
    cos = cos_ref[...]
    sin = sin_ref[...]

    def rotary(t):
        return t * cos + pltpu.roll(t, HEAD_DIM // 2, axis=1) * sin

    R = RET_WIDTH
    pq = _dot(h, w_ref[:, 0:R])
    for hd in range(N_HEADS):
        sl = slice(hd * HEAD_DIM, (hd + 1) * HEAD_DIM)
        q_ref[:, sl] = _bf16(rotary(pq[:, sl]))
    pk = _dot(h, w_ref[:, R:2 * R])
    for hd in range(N_HEADS):
        sl = slice(hd * HEAD_DIM, (hd + 1) * HEAD_DIM)
        k_ref[:, sl] = _bf16(rotary(pk[:, sl]) * (HEAD_DIM ** -0.5))
    v_ref[...] = _bf16(_dot(h, w_ref[:, 2 * R:3 * R]))
    pg = _dot(h, w_ref[:, 3 * R:4 * R])
    sg_ref[...] = _bf16(pg * jax.nn.sigmoid(pg))
    u_ref[...] = _bf16(_dot(h, w_ref[:, 4 * R:]))


def _ln_proj(x2, ln_g, ln_b, w_in_bf, cos_t, sin_t, seq_len):
    n = x2.shape[0]
    tm = TM_PROJ
    tiles_per_seq = seq_len // tm
    row = lambda i: (i, 0)
    const = lambda i: (0, 0)
    out_sd = jax.ShapeDtypeStruct((n, RET_WIDTH), jnp.bfloat16)
    return pl.pallas_call(
        _ln_proj_kernel,
        grid=(n // tm,),
        in_specs=[
            pl.BlockSpec((tm, D_MODEL), row),
            pl.BlockSpec((1, D_MODEL), const),
            pl.BlockSpec((1, D_MODEL), const),
            pl.BlockSpec((D_MODEL, IN_WIDTH), const),
            pl.BlockSpec((tm, HEAD_DIM), lambda i: (i % tiles_per_seq, 0)),
            pl.BlockSpec((tm, HEAD_DIM), lambda i: (i % tiles_per_seq, 0)),
        ],
        out_specs=[pl.BlockSpec((tm, RET_WIDTH), row)] * 5 + [pl.BlockSpec((tm, 128), row)],
        out_shape=[out_sd] * 5 + [jax.ShapeDtypeStruct((n, 128), jnp.float32)],
        compiler_params=pltpu.CompilerParams(
            dimension_semantics=("arbitrary",), vmem_limit_bytes=VMEM_LIMIT),
        name="ln_proj",
    )(x2, ln_g, ln_b, w_in_bf, cos_t, sin_t)


def _log_sigmoid(x):
    return jnp.minimum(x, 0.0) - jnp.log(1.0 + jnp.exp(-jnp.abs(x)))


def _retention_kernel(dlf_ref, dlb_ref, q_ref, k_ref, v_ref, sg_ref, o_ref,
                      dmat, xi_f, xi_b, zeta_f, zeta_b, cdec, s_f, s_b, snap):
    b = pl.program_id(0)
    phase = pl.program_id(1)
    j = pl.program_id(2)
    n_tiles = pl.num_programs(2)
    C = RET_CHUNK
    sub = RET_TILE // C

    @pl.when((b == 0) & (phase == 0) & (j == 0))
    def _init_tables():
        ri = lax.broadcasted_iota(jnp.int32, (C, C), 0).astype(jnp.float32)
        ci = lax.broadcasted_iota(jnp.int32, (C, C), 1).astype(jnp.float32)
        rel = ri - ci
        pos = lax.broadcasted_iota(jnp.int32, (C, HEAD_DIM), 0).astype(jnp.float32)
        zero_cc = jnp.zeros((C, C), jnp.float32)
        zero_cd = jnp.zeros((C, HEAD_DIM), jnp.float32)
        zero_dd = jnp.zeros((HEAD_DIM, HEAD_DIM), jnp.float32)
        for hd in range(N_HEADS):
            dlf = dlf_ref[hd]
            dlb = dlb_ref[hd]
            dmat[hd] = jnp.where(rel >= 0, jnp.exp(_log_sigmoid(zero_cc + dlf) * rel),
                                 jnp.exp(-_log_sigmoid(zero_cc + dlb) * rel))
            lf = _log_sigmoid(zero_cd + dlf)
            lb = _log_sigmoid(zero_cd + dlb)
            xi_f[hd] = jnp.exp(lf * (pos + 1.0))
            xi_b[hd] = jnp.exp(lb * (C - pos))
            zeta_f[hd] = jnp.exp(lf * (C - 1.0 - pos))
            zeta_b[hd] = jnp.exp(lb * pos)
            cdec[hd, 0] = jnp.exp(_log_sigmoid(zero_dd + dlf) * float(C))
            cdec[hd, 1] = jnp.exp(_log_sigmoid(zero_dd + dlb) * float(C))

    @pl.when(j == 0)
    def _reset_state():
        @pl.when(phase == 0)
        def _():
            s_b[...] = jnp.zeros_like(s_b)

        @pl.when(phase == 1)
        def _():
            s_f[...] = jnp.zeros_like(s_f)

    @pl.when(phase == 0)
    def _backward_states():
        tile = n_tiles - 1 - j

        def body(cc, carry):
            c = sub - 1 - cc
            r0 = pl.multiple_of(c * C, C)
            chunk = tile * sub + c
            for hd in range(N_HEADS):
                sl = slice(hd * HEAD_DIM, (hd + 1) * HEAD_DIM)
                kc = k_ref[0, pl.ds(r0, C), sl]
                vc = v_ref[0, pl.ds(r0, C), sl].astype(jnp.float32)
                st = s_b[hd]
                snap[chunk * N_HEADS + hd] = _bf16(st)
                kv = lax.dot_general(kc, _bf16(vc * zeta_b[hd]), _TN,
                                     preferred_element_type=jnp.float32)
                s_b[hd] = st * cdec[hd, 1] + kv
            return carry

        lax.fori_loop(0, sub, body, 0, unroll=True)

    @pl.when(phase == 1)
    def _forward_outputs():
        def body(c, carry):
            r0 = pl.multiple_of(c * C, C)
            chunk = j * sub + c
            for hd in range(N_HEADS):
                sl = slice(hd * HEAD_DIM, (hd + 1) * HEAD_DIM)
                qc = q_ref[0, pl.ds(r0, C), sl]
                kc = k_ref[0, pl.ds(r0, C), sl]
                vc = v_ref[0, pl.ds(r0, C), sl]
                scores = lax.dot_general(qc, kc, _NT, preferred_element_type=jnp.float32)
                o = _dot(_bf16(scores * dmat[hd]), vc)
                st = s_f[hd]
                both = jnp.concatenate([_bf16(st), snap[chunk * N_HEADS + hd]], axis=1)
                cross = _dot(qc, both)
                o = o + xi_f[hd] * cross[:, :HEAD_DIM] + xi_b[hd] * cross[:, HEAD_DIM:]
                kv = lax.dot_general(kc, _bf16(vc.astype(jnp.float32) * zeta_f[hd]), _TN,
                                     preferred_element_type=jnp.float32)
                s_f[hd] = st * cdec[hd, 0] + kv
                mu = jnp.mean(o, axis=-1, keepdims=True)
                oc = o - mu
                on = oc * lax.rsqrt(jnp.mean(oc * oc, axis=-1, keepdims=True) + LN_EPS)
                sg = sg_ref[0, pl.ds(r0, C), sl].astype(jnp.float32)
                o_ref[0, pl.ds(r0, C), sl] = _bf16(sg * on)
            return carry

        lax.fori_loop(0, sub, body, 0, unroll=True)


def _retention(q, k, v, sg, dl_f, dl_b):
    bsz, seq_len, _ = q.shape
    n_tiles = seq_len // RET_TILE
    n_chunks = seq_len // RET_CHUNK
    C = RET_CHUNK

    def kv_map(b, p, j):
        return (b, jnp.where(p == 0, n_tiles - 1 - j, j), 0)

    def q_map(b, p, j):
        return (b, jnp.where(p == 0, 0, j), 0)

    blk = (1, RET_TILE, RET_WIDTH)
    smem = pl.BlockSpec(memory_space=pltpu.SMEM)
    return pl.pallas_call(
        _retention_kernel,
        grid=(bsz, 2, n_tiles),
        in_specs=[smem, smem,
                  pl.BlockSpec(blk, q_map), pl.BlockSpec(blk, kv_map),
                  pl.BlockSpec(blk, kv_map), pl.BlockSpec(blk, q_map)],
        out_specs=pl.BlockSpec(blk, q_map),
        out_shape=jax.ShapeDtypeStruct(q.shape, jnp.bfloat16),
        scratch_shapes=[
            pltpu.VMEM((N_HEADS, C, C), jnp.float32),
            pltpu.VMEM((N_HEADS, C, HEAD_DIM), jnp.float32),
            pltpu.VMEM((N_HEADS, C, HEAD_DIM), jnp.float32),
            pltpu.VMEM((N_HEADS, C, HEAD_DIM), jnp.float32),
            pltpu.VMEM((N_HEADS, C, HEAD_DIM), jnp.float32),
            pltpu.VMEM((N_HEADS, 2, HEAD_DIM, HEAD_DIM), jnp.float32),
            pltpu.VMEM((N_HEADS, HEAD_DIM, HEAD_DIM), jnp.float32),
            pltpu.VMEM((N_HEADS, HEAD_DIM, HEAD_DIM), jnp.float32),
            pltpu.VMEM((n_chunks * N_HEADS, HEAD_DIM, HEAD_DIM), jnp.bfloat16),
        ],
        compiler_params=pltpu.CompilerParams(
            dimension_semantics=("arbitrary", "arbitrary", "arbitrary"),
            vmem_limit_bytes=VMEM_LIMIT),
        name="retention",
    )(dl_f, dl_b, q, k, v, sg)


def _mix_router_kernel(seq_len, x_ref, st_ref, ret_ref, u_ref, up_ref, un_ref, inv_ref,
                       lng_ref, lnb_ref, wpool_ref, pscale_ref, wo_ref,
                       mg_ref, mb_ref, wr_hi_ref, wr_cat_ref, br_ref,
                       h2_ref, h2p_ref, idx_ref, rank_ref, gate_ref, cnt_ref,
                       tri, base):
    i = pl.program_id(0)
    tm = TM_MIX
    sb = MIX_SUB
    n_sub = tm // sb
    halo = POOL_HALO
    tiles_per_seq = seq_len // tm
    ts = i % tiles_per_seq

    @pl.when(i == 0)
    def _init():
        tr = lax.broadcasted_iota(jnp.int32, (sb, sb), 0)
        tc = lax.broadcasted_iota(jnp.int32, (sb, sb), 1)
        tri[...] = jnp.where(tr < tc, 1.0, 0.0).astype(jnp.bfloat16)
        base[...] = jnp.zeros_like(base)

    prev = jnp.where(ts == 0, jnp.zeros_like(up_ref[...]), up_ref[...])
    nxt = jnp.where(ts == tiles_per_seq - 1, jnp.zeros_like(un_ref[...]), un_ref[...])
    e_iota = lax.broadcasted_iota(jnp.int32, (N_EXPERTS, sb), 0)
    k_iota = lax.broadcasted_iota(jnp.int32, (8, sb), 0)
    wr_hi = wr_hi_ref[...]
    wr_cat = wr_cat_ref[...]
    running = base[...]

    for s in range(n_sub):
        r0 = s * sb
        rows = slice(r0, r0 + sb)
        u_cur = u_ref[rows, :]
        above = prev if s == 0 else u_ref[r0 - halo:r0, :]
        below = nxt if s == n_sub - 1 else u_ref[r0 + sb:r0 + sb + halo, :]
        u_ext = jnp.concatenate([above, u_cur, below], axis=0).astype(jnp.float32)
        n_ext = sb + 2 * halo

        sums = []
        for gi, w in enumerate(POOL_WINDOWS):
            a = u_ext[:, gi * POOL_GROUP:(gi + 1) * POOL_GROUP]
            span = 1
            while span < w // 2:
                a = a + pltpu.roll(a, n_ext - span, axis=0)
                span *= 2
            sums.append(a + pltpu.roll(a, w // 2, axis=0))
        wsum = jnp.concatenate(sums, axis=1)[halo:halo + sb, :]
        pooled = wsum * inv_ref[0, rows, :] - u_cur.astype(jnp.float32)
        pool = _bf16(_dot(_bf16(pooled), wpool_ref[...]) * pscale_ref[...])

        m = _dot(ret_ref[rows, :], wo_ref[0:RET_WIDTH, :]) + _dot(pool, wo_ref[RET_WIDTH:, :])
        st = st_ref[rows, :]
        h = (x_ref[rows, :] - st[:, 0:1]) * st[:, 1:2] * lng_ref[...] + lnb_ref[...]
        h2 = _layer_norm(DN_ALPHA * h + m, mg_ref[...], mb_ref[...])
        h2_ref[rows, :] = h2
        h_hi = _bf16(h2)
        h_hi32 = h_hi.astype(jnp.float32)
        h2p_ref[rows, :] = _pack_rounded(h_hi32)

        h_lo = _bf16(h2 - h_hi32)
        part = _dot(h_hi, wr_cat)
        lrow = part + pltpu.roll(part, 128 - N_EXPERTS, axis=1) + _dot(h_lo, wr_hi)
        logits = lrow.T[0:N_EXPERTS, :] + br_ref[...]

        vals8 = jnp.zeros((8, sb), jnp.float32)
        idx8 = jnp.zeros((8, sb), jnp.int32)
        sels = []
        work = logits
        for kk in range(TOP_K):
            mval = jnp.max(work, axis=0, keepdims=True)
            midx = jnp.min(jnp.where(work == mval, e_iota, N_EXPERTS), axis=0, keepdims=True)
            sel = e_iota == midx
            sels.append(sel)
            vals8 = jnp.where(k_iota == kk, mval, vals8)
            idx8 = jnp.where(k_iota == kk, midx, idx8)
            work = jnp.where(sel, -jnp.inf, work)

        ex = jnp.where(k_iota < TOP_K, jnp.exp(vals8 - vals8[0:1, :]), 0.0)
        gates8 = ex / jnp.sum(ex, axis=0, keepdims=True)

        onehot = jnp.zeros((N_EXPERTS, sb), jnp.float32)
        for sel in sels:
            onehot = onehot + jnp.where(sel, 1.0, 0.0)
        cum = _dot(_bf16(onehot), tri[...])
        before = running + cum
        rank8 = jnp.zeros((8, sb), jnp.int32)
        for kk, sel in enumerate(sels):
            rk = jnp.sum(jnp.where(sel, before, 0.0), axis=0, keepdims=True).astype(jnp.int32)
            rank8 = jnp.where(k_iota == kk, rk, rank8)
        total = cum[:, sb - 1:sb] + onehot[:, sb - 1:sb]
        running = running + jnp.broadcast_to(total, (N_EXPERTS, sb))

        idx_ref[:, rows] = idx8
        rank_ref[:, rows] = rank8
        gate_ref[:, rows] = gates8

    base[...] = running
    cnt_ref[...] = running[:, 0:128].astype(jnp.int32)


def _pool_inv_counts(seq_len):
    tm = TM_MIX
    assert seq_len // tm >= 3
    cols = []
    for w in POOL_WINDOWS:
        pos = np.arange(seq_len)
        lo = np.clip(pos - w // 2, 0, seq_len - 1)
        hi = np.clip(pos + (w - 1 - w // 2), 0, seq_len - 1)
        inv = (1.0 / (hi - lo + 1)).astype(np.float32)
        kinds = np.stack([inv[:tm], inv[tm:2 * tm], inv[seq_len - tm:]])
        cols.append(np.repeat(kinds[:, :, None], POOL_GROUP, axis=2))
    return jnp.asarray(np.concatenate(cols, axis=2))


def _mix_router(x2, st, ret, u, ln_g, ln_b, w_pool_bf, pool_scale, w_o_bf, mix_g, mix_b,
                wr_hi, wr_cat, br_full, seq_len):
    n = x2.shape[0]
    tm = TM_MIX
    sb = MIX_SUB
    halo = POOL_HALO
    hb = tm // halo
    n_halo_blocks = n // halo
    tiles_per_seq = seq_len // tm
    row = lambda i: (i, 0)
    const = lambda i: (0, 0)
    col = lambda i: (0, i)

    def inv_map(i):
        ts = i % tiles_per_seq
        return (jnp.where(ts == 0, 0, jnp.where(ts == tiles_per_seq - 1, 2, 1)), 0, 0)

    return pl.pallas_call(
        functools.partial(_mix_router_kernel, seq_len),
        grid=(n // tm,),
        in_specs=[
            pl.BlockSpec((tm, D_MODEL), row),
            pl.BlockSpec((tm, 128), row),
            pl.BlockSpec((tm, RET_WIDTH), row),
            pl.BlockSpec((tm, POOL_WIDTH), row),
            pl.BlockSpec((halo, POOL_WIDTH), lambda i: (jnp.maximum(i * hb - 1, 0), 0)),
            pl.BlockSpec((halo, POOL_WIDTH),
                         lambda i: (jnp.minimum((i + 1) * hb, n_halo_blocks - 1), 0)),
            pl.BlockSpec((1, tm, POOL_WIDTH), inv_map),
            pl.BlockSpec((1, D_MODEL), const),
            pl.BlockSpec((1, D_MODEL), const),
            pl.BlockSpec((POOL_WIDTH, POOL_WIDTH), const),
            pl.BlockSpec((1, POOL_WIDTH), const),
            pl.BlockSpec((D_MODEL, D_MODEL), const),
            pl.BlockSpec((1, D_MODEL), const),
            pl.BlockSpec((1, D_MODEL), const),
            pl.BlockSpec((D_MODEL, 128), const),
            pl.BlockSpec((D_MODEL, 128), const),
            pl.BlockSpec((N_EXPERTS, sb), const),
        ],
        out_specs=[
            pl.BlockSpec((tm, D_MODEL), row),
            pl.BlockSpec((tm, HALF), row),
            pl.BlockSpec((8, tm), col),
            pl.BlockSpec((8, tm), col),
            pl.BlockSpec((8, tm), col),
            pl.BlockSpec((N_EXPERTS, 128), const),
        ],
        out_shape=[
            jax.ShapeDtypeStruct((n, D_MODEL), jnp.float32),
            jax.ShapeDtypeStruct((n, HALF), jnp.int32),
            jax.ShapeDtypeStruct((8, n), jnp.int32),
            jax.ShapeDtypeStruct((8, n), jnp.int32),
            jax.ShapeDtypeStruct((8, n), jnp.float32),
            jax.ShapeDtypeStruct((N_EXPERTS, 128), jnp.int32),
        ],
        scratch_shapes=[
            pltpu.VMEM((sb, sb), jnp.bfloat16),
            pltpu.VMEM((N_EXPERTS, sb), jnp.float32),
        ],
        compiler_params=pltpu.CompilerParams(
            dimension_semantics=("arbitrary",), vmem_limit_bytes=VMEM_LIMIT),
        name="mix_router",
    )(x2, st, ret, u, u, u, _pool_inv_counts(seq_len), ln_g, ln_b, w_pool_bf, pool_scale, w_o_bf,
      mix_g, mix_b, wr_hi, wr_cat, br_full)


def _expert_kernel(it_tile, it_exp, it_rows, it_new, it_slot, it_next,
                   xs_ref, wg_hbm, bg_ref, wu_hbm, bu_ref, wd_hbm, bd_ref, ys_ref,
                   stage, wg_bf, wu_bf, wd_bf, sem):
    i = pl.program_id(0)
    n_valid = it_rows[i]
    w_hbm = (wg_hbm, wu_hbm, wd_hbm)
    w_bf = (wg_bf, wu_bf, wd_bf)

    def weight_copies(expert, slot):
        return [pltpu.make_async_copy(w_hbm[m].at[expert], stage.at[slot, m], sem.at[slot, m])
                for m in range(3)]

    @pl.when(i == 0)
    def _first_fetch():
        for c in weight_copies(it_exp[0], 0):
            c.start()

    @pl.when(it_new[i] == 1)
    def _switch_expert():
        slot = it_slot[i]
        for c in weight_copies(it_exp[i], slot):
            c.wait()
        nxt = it_next[i]

        @pl.when(nxt >= 0)
        def _():
            for c in weight_copies(nxt, 1 - slot):
                c.start()

        for m in range(3):
            w_bf[m][...] = _bf16(stage[slot, m])

    @pl.when(n_valid > 0)
    def _ffn():
        r = lax.broadcasted_iota(jnp.int32, (TM_EXPERT, HALF), 0)
        x_lo, x_hi = _unpack_rows(jnp.where(r < n_valid, xs_ref[...], 0))
        x = jnp.concatenate([_bf16(x_lo), _bf16(x_hi)], axis=1)
        gt = _dot(x, wg_bf[...]) + bg_ref[0]
        up = _dot(x, wu_bf[...]) + bu_ref[0]
        gt = jnp.minimum(gt, SWIGLU_LIMIT)
        up = jnp.clip(up, -SWIGLU_LIMIT, SWIGLU_LIMIT)
        hid = (up + 1.0) * (gt * jax.nn.sigmoid(SWIGLU_ALPHA * gt))
        ys_ref[...] = _pack_rows(_dot(_bf16(hid), wd_bf[...]) + bd_ref[0])


def _experts(items, xs, w_gate, b_gate, w_up, b_up, w_down, b_down):
    it_tile, it_exp, it_rows, it_new, it_slot, it_next = items
    p = xs.shape[0]
    tm = TM_EXPERT
    n_items = it_tile.shape[0]
    tile_map = lambda i, t, e, *_: (t[i], 0)
    b_spec = pl.BlockSpec((1, 1, D_MODEL), lambda i, t, e, *_: (e[i], 0, 0))
    w_spec = pl.BlockSpec(memory_space=pl.ANY)
    w_scratch = pltpu.VMEM((D_MODEL, D_MODEL), jnp.bfloat16)
    grid_spec = pltpu.PrefetchScalarGridSpec(
        num_scalar_prefetch=6,
        grid=(n_items,),
        in_specs=[pl.BlockSpec((tm, HALF), tile_map),
                  w_spec, b_spec, w_spec, b_spec, w_spec, b_spec],
        out_specs=pl.BlockSpec((tm, HALF), tile_map),
        scratch_shapes=[pltpu.VMEM((2, 3, D_MODEL, D_MODEL), jnp.float32),
                        w_scratch, w_scratch, w_scratch,
                        pltpu.SemaphoreType.DMA((2, 3))],
    )
    return pl.pallas_call(
        _expert_kernel,
        grid_spec=grid_spec,
        out_shape=jax.ShapeDtypeStruct((p, HALF), jnp.int32),
        compiler_params=pltpu.CompilerParams(
            dimension_semantics=("arbitrary",), vmem_limit_bytes=VMEM_LIMIT),
        name="experts",
    )(it_tile, it_exp, it_rows, it_new, it_slot, it_next,
      xs, w_gate, b_gate, w_up, b_up, w_down, b_down)


def _padded_tiles(n_rows):
    return n_rows // TM_EXPERT + N_EXPERTS


def _expert_layout(counts, n_rows):
    tm = TM_EXPERT
    n_items = _padded_tiles(n_rows)
    tiles_e = (counts + tm - 1) // tm
    tile_end = jnp.cumsum(tiles_e)
    tile_start = tile_end - tiles_e
    total = tile_end[-1]
    i = jnp.arange(n_items, dtype=jnp.int32)
    tile = jnp.minimum(i, total - 1)
    e = jnp.sum((tile[:, None] >= tile_end[None, :]).astype(jnp.int32), axis=1)
    e = jnp.minimum(e, N_EXPERTS - 1)
    onehot = (e[:, None] == jnp.arange(N_EXPERTS, dtype=jnp.int32)[None, :]).astype(jnp.int32)
    pick = lambda a: jnp.sum(onehot * a[None, :], axis=1)
    first = pick(tile_start)
    valid = i < total
    rows = jnp.where(valid, jnp.clip(pick(counts) - (tile - first) * tm, 0, tm), 0)
    new = valid & (tile == first)
    slot = jnp.maximum(jnp.cumsum(new.astype(jnp.int32)) - 1, 0) % 2
    ids = jnp.arange(N_EXPERTS, dtype=jnp.int32)
    later = jnp.where((counts[None, :] > 0) & (ids[None, :] > ids[:, None]), ids[None, :], N_EXPERTS)
    next_e = jnp.min(later, axis=1)
    nxt = pick(jnp.where(next_e < N_EXPERTS, next_e, -1))
    items = tuple(a.astype(jnp.int32) for a in (tile, e, rows, new, slot, nxt))
    return tile_start * tm, items


def _combine_kernel(h2_ref, g_ref, b_ref, f_ref, o_ref):
    f_lo, f_hi = _unpack_rows(f_ref[...])
    y = DN_ALPHA * h2_ref[...] + jnp.concatenate([f_lo, f_hi], axis=1)
    o_ref[...] = _layer_norm(y, g_ref[...], b_ref[...])


def _combine(h2, ffn_g, ffn_b, f_packed):
    n = h2.shape[0]
    tm = TM_COMBINE
    row = lambda i: (i, 0)
    const = lambda i: (0, 0)
    return pl.pallas_call(
        _combine_kernel,
        grid=(n // tm,),
        in_specs=[
            pl.BlockSpec((tm, D_MODEL), row),
            pl.BlockSpec((1, D_MODEL), const),
            pl.BlockSpec((1, D_MODEL), const),
            pl.BlockSpec((tm, HALF), row),
        ],
        out_specs=pl.BlockSpec((tm, D_MODEL), row),
        out_shape=jax.ShapeDtypeStruct((n, D_MODEL), jnp.float32),
        compiler_params=pltpu.CompilerParams(
            dimension_semantics=("arbitrary",), vmem_limit_bytes=VMEM_LIMIT),
        name="combine",
    )(h2, ffn_g, ffn_b, f_packed)


SC_CORES = 2
SC_SUBCORES = 16
SC_LANES = 16
SC_CHUNK = 64
SC_COMBINE_CHUNK = 32


def _sc_dispatch(slots, h2, n_out_rows):
    n, w = h2.shape
    ch = SC_CHUNK
    n_chunks = n // (SC_CORES * SC_SUBCORES * ch)
    mesh = plsc.VectorSubcoreMesh(core_axis_name="c", subcore_axis_name="s")

    @functools.partial(
        pl.kernel, mesh=mesh,
        out_type=jax.ShapeDtypeStruct((n_out_rows, w), h2.dtype),
        scratch_types=[pltpu.VMEM((TOP_K, ch), jnp.int32), pltpu.VMEM((ch, w), h2.dtype)],
        name="sc_dispatch")
    def k(idx_hbm, h2_hbm, xs_hbm, idx_v, rows_v):
        wid = lax.axis_index("s") * SC_CORES + lax.axis_index("c")

        @pl.loop(0, n_chunks)
        def _(c):
            chunk = wid * n_chunks + c
            for kk in range(TOP_K):
                pltpu.sync_copy(idx_hbm.at[kk, pl.ds(chunk * ch, ch)], idx_v.at[kk])
            pltpu.sync_copy(h2_hbm.at[pl.ds(chunk * ch, ch)], rows_v)
            for kk in range(TOP_K):
                pltpu.sync_copy(rows_v, xs_hbm.at[idx_v.at[kk]])

    return k(slots, h2)


def _sc_combine(slots, gates, ys):
    n = slots.shape[1]
    w = ys.shape[1]
    ch = SC_COMBINE_CHUNK
    lanes = SC_LANES
    n_chunks = n // (SC_CORES * SC_SUBCORES * ch)
    mesh = plsc.VectorSubcoreMesh(core_axis_name="c", subcore_axis_name="s")

    @functools.partial(
        pl.kernel, mesh=mesh,
        out_type=jax.ShapeDtypeStruct((n, w), jnp.int32),
        scratch_types=[pltpu.VMEM((TOP_K, ch), jnp.int32), pltpu.VMEM((TOP_K, ch), jnp.float32),
                       pltpu.VMEM((TOP_K, ch, w), jnp.int32), pltpu.VMEM((ch, w), jnp.int32)],
        compiler_params=dataclasses.replace(pltpu.CompilerParams(), needs_layout_passes=False),
        name="sc_combine")
    def k(idx_hbm, g_hbm, ys_hbm, out_hbm, idx_v, g_v, rows_v, out_v):
        wid = lax.axis_index("s") * SC_CORES + lax.axis_index("c")

        @pl.loop(0, n_chunks)
        def _(c):
            base = (wid * n_chunks + c) * ch
            for kk in range(TOP_K):
                pltpu.sync_copy(idx_hbm.at[kk, pl.ds(base, ch)], idx_v.at[kk])
                pltpu.sync_copy(g_hbm.at[kk, pl.ds(base, ch)], g_v.at[kk])
            for kk in range(TOP_K):
                pltpu.sync_copy(ys_hbm.at[idx_v.at[kk]], rows_v.at[kk])

            @pl.loop(0, ch)
            def _(t):
                tok = jnp.full((lanes,), t, jnp.int32)
                gate = [plsc.load_gather(g_v, [jnp.full((lanes,), kk, jnp.int32), tok])
                        for kk in range(TOP_K)]

                @pl.loop(0, w // lanes)
                def _(j):
                    sl = pl.ds(j * lanes, lanes)
                    acc_lo = jnp.zeros((lanes,), jnp.float32)
                    acc_hi = jnp.zeros((lanes,), jnp.float32)
                    for kk in range(TOP_K):
                        word = rows_v[kk, t, sl]
                        lo = plsc.bitcast(word << 16, jnp.float32)
                        hi = plsc.bitcast(word & jnp.int32(-65536), jnp.float32)
                        acc_lo = acc_lo + gate[kk] * lo
                        acc_hi = acc_hi + gate[kk] * hi
                    packed = plsc.pack(acc_lo, acc_hi, format=plsc.PackFormat.INTERLEAVED)
                    out_v[t, sl] = plsc.bitcast(packed, jnp.int32)

            pltpu.sync_copy(out_v, out_hbm.at[pl.ds(base, ch)])

    return k(slots, gates, ys)


def _rotary_tables(seq_len):
    d = HEAD_DIM
    inv_freq = np.float32(ROPE_BASE) ** (-np.arange(0, d, 2, dtype=np.float32) / np.float32(d))
    ang = np.arange(seq_len, dtype=np.float32)[:, None] * inv_freq[None, :].astype(np.float32)
    cos = np.cos(ang).astype(np.float32)
    sin = np.sin(ang).astype(np.float32)
    return (jnp.asarray(np.concatenate([cos, cos], axis=1)),
            jnp.asarray(np.concatenate([-sin, sin], axis=1)))


def _trunk(x, p):
    bsz, seq_len, _ = x.shape
    n = bsz * seq_len
    x2 = x.reshape(n, D_MODEL)
    cos_t, sin_t = _rotary_tables(seq_len)

    q, k, v, sg, u, st = _ln_proj(x2, p["ln_in_g"], p["ln_in_b"], p["w_in"], cos_t, sin_t, seq_len)
    shp = (bsz, seq_len, RET_WIDTH)
    ret = _retention(q.reshape(shp), k.reshape(shp), v.reshape(shp), sg.reshape(shp),
                     p["dl_f"], p["dl_b"]).reshape(n, RET_WIDTH)

    h2, h2p, idx8, rank8, gates8, cnt = _mix_router(
        x2, st, ret, u, p["ln_in_g"], p["ln_in_b"], p["w_pool"], p["pool_scale"], p["w_o"],
        p["ln_mix_g"], p["ln_mix_b"], p["wr_hi"], p["wr_cat"], p["br_full"], seq_len)

    starts, items = _expert_layout(cnt[:, 0], n * TOP_K)
    idx = idx8[:TOP_K]
    eq = idx[None, :, :] == jnp.arange(N_EXPERTS, dtype=jnp.int32)[:, None, None]
    slots = rank8[:TOP_K] + jnp.sum(jnp.where(eq, starts[:, None, None], 0), axis=0)

    xs = _sc_dispatch(slots, h2p, _padded_tiles(n * TOP_K) * TM_EXPERT)
    ys = _experts(items, xs, p["w_gate"], p["b_gate"], p["w_up"], p["b_up"],
                  p["w_down"], p["b_down"])
    f_packed = _sc_combine(slots, gates8, ys)
    out = _combine(h2, p["ln_ffn_g"], p["ln_ffn_b"], f_packed)
    return out.reshape(bsz, seq_len, D_MODEL)


def kernel(x_prompt, x_sample, ln_in_g, ln_in_b, w_in, decay_logit_fwd, decay_logit_bwd, w_pool,
           pool_scale, w_o, ln_mix_g, ln_mix_b, w_router, b_router, w_gate, b_gate, w_up, b_up,
           w_down, b_down, ln_ffn_g, ln_ffn_b):
    row = lambda a: a.reshape(1, -1)
    wr = w_router[0]
    wr_hi = _bf16(wr)
    wr_lo = _bf16(wr - wr_hi.astype(jnp.float32))
    lane_pad = jnp.zeros((D_MODEL, 128 - 2 * N_EXPERTS), jnp.bfloat16)
    wr_cat = jnp.concatenate([wr_hi, wr_lo, lane_pad], axis=1)
    wr_hi = jnp.concatenate([wr_hi, jnp.zeros_like(wr_hi), lane_pad], axis=1)
    eye = jnp.eye(len(POOL_WINDOWS), dtype=jnp.float32)
    w_pool_bd = (eye[:, None, :, None] * w_pool[0][:, :, None, :]).reshape(POOL_WIDTH, POOL_WIDTH)
    p = {
        "ln_in_g": row(ln_in_g), "ln_in_b": row(ln_in_b),
        "w_in": _bf16(w_in[0]),
        "dl_f": decay_logit_fwd[0], "dl_b": decay_logit_bwd[0],
        "w_pool": _bf16(w_pool_bd), "pool_scale": row(pool_scale[0]),
        "w_o": _bf16(w_o[0]),
        "ln_mix_g": row(ln_mix_g[0]), "ln_mix_b": row(ln_mix_b[0]),
        "wr_hi": wr_hi, "wr_cat": wr_cat,
        "br_full": jnp.broadcast_to(b_router[0][:, None], (N_EXPERTS, MIX_SUB)),
        "w_gate": w_gate[0], "b_gate": b_gate[0][:, None, :],
        "w_up": w_up[0], "b_up": b_up[0][:, None, :],
        "w_down": w_down[0], "b_down": b_down[0][:, None, :],
        "ln_ffn_g": row(ln_ffn_g[0]), "ln_ffn_b": row(ln_ffn_b[0]),
    }
    return (_trunk(x_prompt, p), _trunk(x_sample, p))
```

```python
import dataclasses
import functools

import jax
import jax.numpy as jnp
import numpy as np
from jax import lax
from jax.experimental import pallas as pl
from jax.experimental.pallas import tpu as pltpu
from jax.experimental.pallas import tpu_sc as plsc

D_MODEL = 1024
RET_WIDTH = 512
POOL_WIDTH = 512
N_HEADS = 4
HEAD_DIM = 128
ROPE_BASE = 10000.0
POOL_WINDOWS = (2, 4, 8, 16)
POOL_GROUP = 128
IN_WIDTH = 4 * RET_WIDTH + POOL_WIDTH
N_EXPERTS = 32
TOP_K = 4
SWIGLU_ALPHA = 1.702
SWIGLU_LIMIT = 7.0
LN_EPS = 1e-5
DN_ALPHA = 2.0 ** 0.25

TM_PROJ = 1024
RET_TILE = 2048
RET_CHUNK = 256
TM_MIX = 1024
MIX_SUB = 1024
POOL_HALO = 16
TM_EXPERT = 512
TM_COMBINE = 512

VMEM_LIMIT = 56 * 1024 * 1024

_NT = (((1,), (1,)), ((), ()))
_TN = (((0,), (0,)), ((), ()))


def _ln_stats(x):
    mu = jnp.mean(x, axis=-1, keepdims=True)
    xc = x - mu
    var = jnp.mean(xc * xc, axis=-1, keepdims=True)
    return mu, lax.rsqrt(var + LN_EPS)


def _layer_norm(x, g, b):
    mu, rstd = _ln_stats(x)
    return (x - mu) * rstd * g + b


def _bf16(x):
    return x.astype(jnp.bfloat16)


def _dot(a, b):
    return jnp.dot(a, b, preferred_element_type=jnp.float32)


HALF = D_MODEL // 2


def _pack_rows(x):
    return _pack_rounded(_bf16(x).astype(jnp.float32))


def _pack_rounded(xr):
    lo = lax.bitcast_convert_type(xr[:, :HALF], jnp.uint32)
    hi = lax.bitcast_convert_type(xr[:, HALF:], jnp.uint32)
    return lax.bitcast_convert_type((lo >> 16) | hi, jnp.int32)


def _unpack_rows(w):
    u = lax.bitcast_convert_type(w, jnp.uint32)
    lo = lax.bitcast_convert_type(u << 16, jnp.float32)
    hi = lax.bitcast_convert_type(u & jnp.uint32(0xFFFF0000), jnp.float32)
    return lo, hi


def _ln_proj_kernel(x_ref, g_ref, b_ref, w_ref, cos_ref, sin_ref,
                    q_ref, k_ref, v_ref, sg_ref, u_ref, st_ref):
    x = x_ref[...]
    mu, rstd = _ln_stats(x)
    lane = lax.broadcasted_iota(jnp.int32, st_ref.shape, 1)
    st_ref[...] = jnp.where(lane == 0, mu, jnp.where(lane == 1, rstd, 0.0))
    h = _bf16((x - mu) * rstd * g_ref[...] + b_ref[...])
    cos = cos_ref[...]
    sin = sin_ref[...]

    def rotary(t):
        return t * cos + pltpu.roll(t, HEAD_DIM // 2, axis=1) * sin

    R = RET_WIDTH
    pq = _dot(h, w_ref[:, 0:R])
    for hd in range(N_HEADS):
        sl = slice(hd * HEAD_DIM, (hd + 1) * HEAD_DIM)
        q_ref[:, sl] = _bf16(rotary(pq[:, sl]))
    pk = _dot(h, w_ref[:, R:2 * R])
    for hd in range(N_HEADS):
        sl = slice(hd * HEAD_DIM, (hd + 1) * HEAD_DIM)
        k_ref[:, sl] = _bf16(rotary(pk[:, sl]) * (HEAD_DIM ** -0.5))
    v_ref[...] = _bf16(_dot(h, w_ref[:, 2 * R:3 * R]))
    pg = _dot(h, w_ref[:, 3 * R:4 * R])
    sg_ref[...] = _bf16(pg * jax.nn.sigmoid(pg))
    u_ref[...] = _bf16(_dot(h, w_ref[:, 4 * R:]))


def _ln_proj(x2, ln_g, ln_b, w_in_bf, cos_t, sin_t, seq_len):
    n = x2.shape[0]
    tm = TM_PROJ
    tiles_per_seq = seq_len // tm
    row = lambda i: (i, 0)
    const = lambda i: (0, 0)
    out_sd = jax.ShapeDtypeStruct((n, RET_WIDTH), jnp.bfloat16)
    return pl.pallas_call(
        _ln_proj_kernel,
        grid=(n // tm,),
        in_specs=[
            pl.BlockSpec((tm, D_MODEL), row),
            pl.BlockSpec((1, D_MODEL), const),
            pl.BlockSpec((1, D_MODEL), const),
            pl.BlockSpec((D_MODEL, IN_WIDTH), const),
            pl.BlockSpec((tm, HEAD_DIM), lambda i: (i % tiles_per_seq, 0)),
            pl.BlockSpec((tm, HEAD_DIM), lambda i: (i % tiles_per_seq, 0)),
        ],
        out_specs=[pl.BlockSpec((tm, RET_WIDTH), row)] * 5 + [pl.BlockSpec((tm, 128), row)],
        out_shape=[out_sd] * 5 + [jax.ShapeDtypeStruct((n, 128), jnp.float32)],
        compiler_params=pltpu.CompilerParams(
            dimension_semantics=("arbitrary",), vmem_limit_bytes=VMEM_LIMIT),
        name="ln_proj",
    )(x2, ln_g, ln_b, w_in_bf, cos_t, sin_t)


def _log_sigmoid(x):
    return jnp.minimum(x, 0.0) - jnp.log(1.0 + jnp.exp(-jnp.abs(x)))


def _retention_kernel(dlf_ref, dlb_ref, q_ref, k_ref, v_ref, sg_ref, o_ref,
                      dmat, xi_f, xi_b, zeta_f, zeta_b, cdec, s_f, s_b, snap):
    b = pl.program_id(0)
    phase = pl.program_id(1)
    j = pl.program_id(2)
    n_tiles = pl.num_programs(2)
    C = RET_CHUNK
    sub = RET_TILE // C

    @pl.when((b == 0) & (phase == 0) & (j == 0))
    def _init_tables():
        ri = lax.broadcasted_iota(jnp.int32, (C, C), 0).astype(jnp.float32)
        ci = lax.broadcasted_iota(jnp.int32, (C, C), 1).astype(jnp.float32)
        rel = ri - ci
        pos = lax.broadcasted_iota(jnp.int32, (C, HEAD_DIM), 0).astype(jnp.float32)
        zero_cc = jnp.zeros((C, C), jnp.float32)
        zero_cd = jnp.zeros((C, HEAD_DIM), jnp.float32)
        zero_dd = jnp.zeros((HEAD_DIM, HEAD_DIM), jnp.float32)
        for hd in range(N_HEADS):
            dlf = dlf_ref[hd]
            dlb = dlb_ref[hd]
            dmat[hd] = jnp.where(rel >= 0, jnp.exp(_log_sigmoid(zero_cc + dlf) * rel),
                                 jnp.exp(-_log_sigmoid(zero_cc + dlb) * rel))
            lf = _log_sigmoid(zero_cd + dlf)
            lb = _log_sigmoid(zero_cd + dlb)
            xi_f[hd] = jnp.exp(lf * (pos + 1.0))
            xi_b[hd] = jnp.exp(lb * (C - pos))
            zeta_f[hd] = jnp.exp(lf * (C - 1.0 - pos))
            zeta_b[hd] = jnp.exp(lb * pos)
            cdec[hd, 0] = jnp.exp(_log_sigmoid(zero_dd + dlf) * float(C))
            cdec[hd, 1] = jnp.exp(_log_sigmoid(zero_dd + dlb) * float(C))

    @pl.when(j == 0)
    def _reset_state():
        @pl.when(phase == 0)
        def _():
            s_b[...] = jnp.zeros_like(s_b)

        @pl.when(phase == 1)
        def _():
            s_f[...] = jnp.zeros_like(s_f)

    @pl.when(phase == 0)
    def _backward_states():
        tile = n_tiles - 1 - j

        def body(cc, carry):
            c = sub - 1 - cc
            r0 = pl.multiple_of(c * C, C)
            chunk = tile * sub + c
            for hd in range(N_HEADS):
                sl = slice(hd * HEAD_DIM, (hd + 1) * HEAD_DIM)
                kc = k_ref[0, pl.ds(r0, C), sl]
                vc = v_ref[0, pl.ds(r0, C), sl].astype(jnp.float32)
                st = s_b[hd]
                snap[chunk * N_HEADS + hd] = _bf16(st)
                kv = lax.dot_general(kc, _bf16(vc * zeta_b[hd]), _TN,
                                     preferred_element_type=jnp.float32)
                s_b[hd] = st * cdec[hd, 1] + kv
            return carry

        lax.fori_loop(0, sub, body, 0, unroll=True)

    @pl.when(phase == 1)
    def _forward_outputs():
        def body(c, carry):
            r0 = pl.multiple_of(c * C, C)
            chunk = j * sub + c
            for hd in range(N_HEADS):
                sl = slice(hd * HEAD_DIM, (hd + 1) * HEAD_DIM)
                qc = q_ref[0, pl.ds(r0, C), sl]
                kc = k_ref[0, pl.ds(r0, C), sl]
                vc = v_ref[0, pl.ds(r0, C), sl]
                scores = lax.dot_general(qc, kc, _NT, preferred_element_type=jnp.float32)
                o = _dot(_bf16(scores * dmat[hd]), vc)
                st = s_f[hd]
                both = jnp.concatenate([_bf16(st), snap[chunk * N_HEADS + hd]], axis=1)
                cross = _dot(qc, both)
                o = o + xi_f[hd] * cross[:, :HEAD_DIM] + xi_b[hd] * cross[:, HEAD_DIM:]
                kv = lax.dot_general(kc, _bf16(vc.astype(jnp.float32) * zeta_f[hd]), _TN,
                                     preferred_element_type=jnp.float32)
                s_f[hd] = st * cdec[hd, 0] + kv
                mu = jnp.mean(o, axis=-1, keepdims=True)
                oc = o - mu
                on = oc * lax.rsqrt(jnp.mean(oc * oc, axis=-1, keepdims=True) + LN_EPS)
                sg = sg_ref[0, pl.ds(r0, C), sl].astype(jnp.float32)
                o_ref[0, pl.ds(r0, C), sl] = _bf16(sg * on)
            return carry

        lax.fori_loop(0, sub, body, 0, unroll=True)


def _retention(q, k, v, sg, dl_f, dl_b):
    bsz, seq_len, _ = q.shape
    n_tiles = seq_len // RET_TILE
    n_chunks = seq_len // RET_CHUNK
    C = RET_CHUNK

    def kv_map(b, p, j):
        return (b, jnp.where(p == 0, n_tiles - 1 - j, j), 0)

    def q_map(b, p, j):
        return (b, jnp.where(p == 0, 0, j), 0)

    blk = (1, RET_TILE, RET_WIDTH)
    smem = pl.BlockSpec(memory_space=pltpu.SMEM)
    return pl.pallas_call(
        _retention_kernel,
        grid=(bsz, 2, n_tiles),
        in_specs=[smem, smem,
                  pl.BlockSpec(blk, q_map), pl.BlockSpec(blk, kv_map),
                  pl.BlockSpec(blk, kv_map), pl.BlockSpec(blk, q_map)],
        out_specs=pl.BlockSpec(blk, q_map),
        out_shape=jax.ShapeDtypeStruct(q.shape, jnp.bfloat16),
        scratch_shapes=[
            pltpu.VMEM((N_HEADS, C, C), jnp.float32),
            pltpu.VMEM((N_HEADS, C, HEAD_DIM), jnp.float32),
            pltpu.VMEM((N_HEADS, C, HEAD_DIM), jnp.float32),
            pltpu.VMEM((N_HEADS, C, HEAD_DIM), jnp.float32),
            pltpu.VMEM((N_HEADS, C, HEAD_DIM), jnp.float32),
            pltpu.VMEM((N_HEADS, 2, HEAD_DIM, HEAD_DIM), jnp.float32),
            pltpu.VMEM((N_HEADS, HEAD_DIM, HEAD_DIM), jnp.float32),
            pltpu.VMEM((N_HEADS, HEAD_DIM, HEAD_DIM), jnp.float32),
            pltpu.VMEM((n_chunks * N_HEADS, HEAD_DIM, HEAD_DIM), jnp.bfloat16),
        ],
        compiler_params=pltpu.CompilerParams(
            dimension_semantics=("arbitrary", "arbitrary", "arbitrary"),
            vmem_limit_bytes=VMEM_LIMIT),
        name="retention",
    )(dl_f, dl_b, q, k, v, sg)


def _mix_router_kernel(seq_len, x_ref, st_ref, ret_ref, u_ref, up_ref, un_ref, inv_ref,
                       lng_ref, lnb_ref, wpool_ref, pscale_ref, wo_ref,
                       mg_ref, mb_ref, wr_hi_ref, wr_cat_ref, br_ref,
                       h2_ref, h2p_ref, idx_ref, rank_ref, gate_ref, cnt_ref,
                       tri, base):
    i = pl.program_id(0)
    tm = TM_MIX
    sb = MIX_SUB
    n_sub = tm // sb
    halo = POOL_HALO
    tiles_per_seq = seq_len // tm
    ts = i % tiles_per_seq

    @pl.when(i == 0)
    def _init():
        tr = lax.broadcasted_iota(jnp.int32, (sb, sb), 0)
        tc = lax.broadcasted_iota(jnp.int32, (sb, sb), 1)
        tri[...] = jnp.where(tr < tc, 1.0, 0.0).astype(jnp.bfloat16)
        base[...] = jnp.zeros_like(base)

    prev = jnp.where(ts == 0, jnp.zeros_like(up_ref[...]), up_ref[...])
    nxt = jnp.where(ts == tiles_per_seq - 1, jnp.zeros_like(un_ref[...]), un_ref[...])
    e_iota = lax.broadcasted_iota(jnp.int32, (N_EXPERTS, sb), 0)
    k_iota = lax.broadcasted_iota(jnp.int32, (8, sb), 0)
    wr_hi = wr_hi_ref[...]
    wr_cat = wr_cat_ref[...]
    running = base[...]

    for s in range(n_sub):
        r0 = s * sb
        rows = slice(r0, r0 + sb)
        u_cur = u_ref[rows, :]
        above = prev if s == 0 else u_ref[r0 - halo:r0, :]
        below = nxt if s == n_sub - 1 else u_ref[r0 + sb:r0 + sb + halo, :]
        u_ext = jnp.concatenate([above, u_cur, below], axis=0).astype(jnp.float32)
        n_ext = sb + 2 * halo

        sums = []
        for gi, w in enumerate(POOL_WINDOWS):
            a = u_ext[:, gi * POOL_GROUP:(gi + 1) * POOL_GROUP]
            span = 1
            while span < w // 2:
                a = a + pltpu.roll(a, n_ext - span, axis=0)
                span *= 2
            sums.append(a + pltpu.roll(a, w // 2, axis=0))
        wsum = jnp.concatenate(sums, axis=1)[halo:halo + sb, :]
        pooled = wsum * inv_ref[0, rows, :] - u_cur.astype(jnp.float32)
        pool = _bf16(_dot(_bf16(pooled), wpool_ref[...]) * pscale_ref[...])

        m = _dot(ret_ref[rows, :], wo_ref[0:RET_WIDTH, :]) + _dot(pool, wo_ref[RET_WIDTH:, :])
        st = st_ref[rows, :]
        h = (x_ref[rows, :] - st[:, 0:1]) * st[:, 1:2] * lng_ref[...] + lnb_ref[...]
        h2 = _layer_norm(DN_ALPHA * h + m, mg_ref[...], mb_ref[...])
        h2_ref[rows, :] = h2
        h_hi = _bf16(h2)
        h_hi32 = h_hi.astype(jnp.float32)
        h2p_ref[rows, :] = _pack_rounded(h_hi32)

        h_lo = _bf16(h2 - h_hi32)
        part = _dot(h_hi, wr_cat)
        lrow = part + pltpu.roll(part, 128 - N_EXPERTS, axis=1) + _dot(h_lo, wr_hi)
        logits = lrow.T[0:N_EXPERTS, :] + br_ref[...]

        vals8 = jnp.zeros((8, sb), jnp.float32)
        idx8 = jnp.zeros((8, sb), jnp.int32)
        sels = []
        work = logits
        for kk in range(TOP_K):
            mval = jnp.max(work, axis=0, keepdims=True)
            midx = jnp.min(jnp.where(work == mval, e_iota, N_EXPERTS), axis=0, keepdims=True)
            sel = e_iota == midx
            sels.append(sel)
            vals8 = jnp.where(k_iota == kk, mval, vals8)
            idx8 = jnp.where(k_iota == kk, midx, idx8)
            work = jnp.where(sel, -jnp.inf, work)

        ex = jnp.where(k_iota < TOP_K, jnp.exp(vals8 - vals8[0:1, :]), 0.0)
        gates8 = ex / jnp.sum(ex, axis=0, keepdims=True)

        onehot = jnp.zeros((N_EXPERTS, sb), jnp.float32)
        for sel in sels:
            onehot = onehot + jnp.where(sel, 1.0, 0.0)
        cum = _dot(_bf16(onehot), tri[...])
        before = running + cum
        rank8 = jnp.zeros((8, sb), jnp.int32)
        for kk, sel in enumerate(sels):
            rk = jnp.sum(jnp.where(sel, before, 0.0), axis=0, keepdims=True).astype(jnp.int32)
            rank8 = jnp.where(k_iota == kk, rk, rank8)
        total = cum[:, sb - 1:sb] + onehot[:, sb - 1:sb]
        running = running + jnp.broadcast_to(total, (N_EXPERTS, sb))

        idx_ref[:, rows] = idx8
        rank_ref[:, rows] = rank8
        gate_ref[:, rows] = gates8

    base[...] = running
    cnt_ref[...] = running[:, 0:128].astype(jnp.int32)


def _pool_inv_counts(seq_len):
    tm = TM_MIX
    assert seq_len // tm >= 3
    cols = []
    for w in POOL_WINDOWS:
        pos = np.arange(seq_len)
        lo = np.clip(pos - w // 2, 0, seq_len - 1)
        hi = np.clip(pos + (w - 1 - w // 2), 0, seq_len - 1)
        inv = (1.0 / (hi - lo + 1)).astype(np.float32)
        kinds = np.stack([inv[:tm], inv[tm:2 * tm], inv[seq_len - tm:]])
        cols.append(np.repeat(kinds[:, :, None], POOL_GROUP, axis=2))
    return jnp.asarray(np.concatenate(cols, axis=2))


def _mix_router(x2, st, ret, u, ln_g, ln_b, w_pool_bf, pool_scale, w_o_bf, mix_g, mix_b,
                wr_hi, wr_cat, br_full, seq_len):
    n = x2.shape[0]
    tm = TM_MIX
    sb = MIX_SUB
    halo = POOL_HALO
    hb = tm // halo
    n_halo_blocks = n // halo
    tiles_per_seq = seq_len // tm
    row = lambda i: (i, 0)
    const = lambda i: (0, 0)
    col = lambda i: (0, i)

    def inv_map(i):
        ts = i % tiles_per_seq
        return (jnp.where(ts == 0, 0, jnp.where(ts == tiles_per_seq - 1, 2, 1)), 0, 0)

    return pl.pallas_call(
        functools.partial(_mix_router_kernel, seq_len),
        grid=(n // tm,),
        in_specs=[
            pl.BlockSpec((tm, D_MODEL), row),
            pl.BlockSpec((tm, 128), row),
            pl.BlockSpec((tm, RET_WIDTH), row),
            pl.BlockSpec((tm, POOL_WIDTH), row),
            pl.BlockSpec((halo, POOL_WIDTH), lambda i: (jnp.maximum(i * hb - 1, 0), 0)),
            pl.BlockSpec((halo, POOL_WIDTH),
                         lambda i: (jnp.minimum((i + 1) * hb, n_halo_blocks - 1), 0)),
            pl.BlockSpec((1, tm, POOL_WIDTH), inv_map),
            pl.BlockSpec((1, D_MODEL), const),
            pl.BlockSpec((1, D_MODEL), const),
            pl.BlockSpec((POOL_WIDTH, POOL_WIDTH), const),
            pl.BlockSpec((1, POOL_WIDTH), const),
            pl.BlockSpec((D_MODEL, D_MODEL), const),
            pl.BlockSpec((1, D_MODEL), const),
            pl.BlockSpec((1, D_MODEL), const),
            pl.BlockSpec((D_MODEL, 128), const),
            pl.BlockSpec((D_MODEL, 128), const),
            pl.BlockSpec((N_EXPERTS, sb), const),
        ],
        out_specs=[
            pl.BlockSpec((tm, D_MODEL), row),
            pl.BlockSpec((tm, HALF), row),
            pl.BlockSpec((8, tm), col),
            pl.BlockSpec((8, tm), col),
            pl.BlockSpec((8, tm), col),
            pl.BlockSpec((N_EXPERTS, 128), const),
        ],
        out_shape=[
            jax.ShapeDtypeStruct((n, D_MODEL), jnp.float32),
            jax.ShapeDtypeStruct((n, HALF), jnp.int32),
            jax.ShapeDtypeStruct((8, n), jnp.int32),
            jax.ShapeDtypeStruct((8, n), jnp.int32),
            jax.ShapeDtypeStruct((8, n), jnp.float32),
            jax.ShapeDtypeStruct((N_EXPERTS, 128), jnp.int32),
        ],
        scratch_shapes=[
            pltpu.VMEM((sb, sb), jnp.bfloat16),
            pltpu.VMEM((N_EXPERTS, sb), jnp.float32),
        ],
        compiler_params=pltpu.CompilerParams(
            dimension_semantics=("arbitrary",), vmem_limit_bytes=VMEM_LIMIT),
        name="mix_router",
    )(x2, st, ret, u, u, u, _pool_inv_counts(seq_len), ln_g, ln_b, w_pool_bf, pool_scale, w_o_bf,
      mix_g, mix_b, wr_hi, wr_cat, br_full)


def _expert_kernel(it_tile, it_exp, it_rows, it_new, it_slot, it_next,
                   xs_ref, wg_hbm, bg_ref, wu_hbm, bu_ref, wd_hbm, bd_ref, ys_ref,
                   stage, wg_bf, wu_bf, wd_bf, sem):
    i = pl.program_id(0)
    n_valid = it_rows[i]
    w_hbm = (wg_hbm, wu_hbm, wd_hbm)
    w_bf = (wg_bf, wu_bf, wd_bf)

    def weight_copies(expert, slot):
        return [pltpu.make_async_copy(w_hbm[m].at[expert], stage.at[slot, m], sem.at[slot, m])
                for m in range(3)]

    @pl.when(i == 0)
    def _first_fetch():
        for c in weight_copies(it_exp[0], 0):
            c.start()

    @pl.when(it_new[i] == 1)
    def _switch_expert():
        slot = it_slot[i]
        for c in weight_copies(it_exp[i], slot):
            c.wait()
        nxt = it_next[i]

        @pl.when(nxt >= 0)
        def _():
            for c in weight_copies(nxt, 1 - slot):
                c.start()

        for m in range(3):
            w_bf[m][...] = _bf16(stage[slot, m])

    @pl.when(n_valid > 0)
    def _ffn():
        r = lax.broadcasted_iota(jnp.int32, (TM_EXPERT, HALF), 0)
        x_lo, x_hi = _unpack_rows(jnp.where(r < n_valid, xs_ref[...], 0))
        x = jnp.concatenate([_bf16(x_lo), _bf16(x_hi)], axis=1)
        gt = _dot(x, wg_bf[...]) + bg_ref[0]
        up = _dot(x, wu_bf[...]) + bu_ref[0]
        gt = jnp.minimum(gt, SWIGLU_LIMIT)
        up = jnp.clip(up, -SWIGLU_LIMIT, SWIGLU_LIMIT)
        hid = (up + 1.0) * (gt * jax.nn.sigmoid(SWIGLU_ALPHA * gt))
        ys_ref[...] = _pack_rows(_dot(_bf16(hid), wd_bf[...]) + bd_ref[0])


def _experts(items, xs, w_gate, b_gate, w_up, b_up, w_down, b_down):
    it_tile, it_exp, it_rows, it_new, it_slot, it_next = items
    p = xs.shape[0]
    tm = TM_EXPERT
    n_items = it_tile.shape[0]
    tile_map = lambda i, t, e, *_: (t[i], 0)
    b_spec = pl.BlockSpec((1, 1, D_MODEL), lambda i, t, e, *_: (e[i], 0, 0))
    w_spec = pl.BlockSpec(memory_space=pl.ANY)
    w_scratch = pltpu.VMEM((D_MODEL, D_MODEL), jnp.bfloat16)
    grid_spec = pltpu.PrefetchScalarGridSpec(
        num_scalar_prefetch=6,
        grid=(n_items,),
        in_specs=[pl.BlockSpec((tm, HALF), tile_map),
                  w_spec, b_spec, w_spec, b_spec, w_spec, b_spec],
        out_specs=pl.BlockSpec((tm, HALF), tile_map),
        scratch_shapes=[pltpu.VMEM((2, 3, D_MODEL, D_MODEL), jnp.float32),
                        w_scratch, w_scratch, w_scratch,
                        pltpu.SemaphoreType.DMA((2, 3))],
    )
    return pl.pallas_call(
        _expert_kernel,
        grid_spec=grid_spec,
        out_shape=jax.ShapeDtypeStruct((p, HALF), jnp.int32),
        compiler_params=pltpu.CompilerParams(
            dimension_semantics=("arbitrary",), vmem_limit_bytes=VMEM_LIMIT),
        name="experts",
    )(it_tile, it_exp, it_rows, it_new, it_slot, it_next,
      xs, w_gate, b_gate, w_up, b_up, w_down, b_down)


def _padded_tiles(n_rows):
    return n_rows // TM_EXPERT + N_EXPERTS


def _expert_layout(counts, n_rows):
    tm = TM_EXPERT
    n_items = _padded_tiles(n_rows)
    tiles_e = (counts + tm - 1) // tm
    tile_end = jnp.cumsum(tiles_e)
    tile_start = tile_end - tiles_e
    total = tile_end[-1]
    i = jnp.arange(n_items, dtype=jnp.int32)
    tile = jnp.minimum(i, total - 1)
    e = jnp.sum((tile[:, None] >= tile_end[None, :]).astype(jnp.int32), axis=1)
    e = jnp.minimum(e, N_EXPERTS - 1)
    onehot = (e[:, None] == jnp.arange(N_EXPERTS, dtype=jnp.int32)[None, :]).astype(jnp.int32)
    pick = lambda a: jnp.sum(onehot * a[None, :], axis=1)
    first = pick(tile_start)
    valid = i < total
    rows = jnp.where(valid, jnp.clip(pick(counts) - (tile - first) * tm, 0, tm), 0)
    new = valid & (tile == first)
    slot = jnp.maximum(jnp.cumsum(new.astype(jnp.int32)) - 1, 0) % 2
    ids = jnp.arange(N_EXPERTS, dtype=jnp.int32)
    later = jnp.where((counts[None, :] > 0) & (ids[None, :] > ids[:, None]), ids[None, :], N_EXPERTS)
    next_e = jnp.min(later, axis=1)
    nxt = pick(jnp.where(next_e < N_EXPERTS, next_e, -1))
    items = tuple(a.astype(jnp.int32) for a in (tile, e, rows, new, slot, nxt))
    return tile_start * tm, items


def _combine_kernel(h2_ref, g_ref, b_ref, f_ref, o_ref):
    f_lo, f_hi = _unpack_rows(f_ref[...])
    y = DN_ALPHA * h2_ref[...] + jnp.concatenate([f_lo, f_hi], axis=1)
    o_ref[...] = _layer_norm(y, g_ref[...], b_ref[...])


def _combine(h2, ffn_g, ffn_b, f_packed):
    n = h2.shape[0]
    tm = TM_COMBINE
    row = lambda i: (i, 0)
    const = lambda i: (0, 0)
    return pl.pallas_call(
        _combine_kernel,
        grid=(n // tm,),
        in_specs=[
            pl.BlockSpec((tm, D_MODEL), row),
            pl.BlockSpec((1, D_MODEL), const),
            pl.BlockSpec((1, D_MODEL), const),
            pl.BlockSpec((tm, HALF), row),
        ],
        out_specs=pl.BlockSpec((tm, D_MODEL), row),
        out_shape=jax.ShapeDtypeStruct((n, D_MODEL), jnp.float32),
        compiler_params=pltpu.CompilerParams(
            dimension_semantics=("arbitrary",), vmem_limit_bytes=VMEM_LIMIT),
        name="combine",
    )(h2, ffn_g, ffn_b, f_packed)


SC_CORES = 2
SC_SUBCORES = 16
SC_LANES = 16
SC_CHUNK = 64
SC_COMBINE_CHUNK = 16
SC_COMBINE_UNROLL = 4


def _sc_dispatch(slots, h2, n_out_rows):
    n, w = h2.shape
    ch = SC_CHUNK
    n_chunks = n // (SC_CORES * SC_SUBCORES * ch)
    mesh = plsc.VectorSubcoreMesh(core_axis_name="c", subcore_axis_name="s")

    @functools.partial(
        pl.kernel, mesh=mesh,
        out_type=jax.ShapeDtypeStruct((n_out_rows, w), h2.dtype),
        scratch_types=[pltpu.VMEM((TOP_K, ch), jnp.int32), pltpu.VMEM((ch, w), h2.dtype)],
        name="sc_dispatch")
    def k(idx_hbm, h2_hbm, xs_hbm, idx_v, rows_v):
        wid = lax.axis_index("s") * SC_CORES + lax.axis_index("c")

        @pl.loop(0, n_chunks)
        def _(c):
            chunk = wid * n_chunks + c
            for kk in range(TOP_K):
                pltpu.sync_copy(idx_hbm.at[kk, pl.ds(chunk * ch, ch)], idx_v.at[kk])
            pltpu.sync_copy(h2_hbm.at[pl.ds(chunk * ch, ch)], rows_v)
            for kk in range(TOP_K):
                pltpu.sync_copy(rows_v, xs_hbm.at[idx_v.at[kk]])

    return k(slots, h2)


def _sc_combine(slots, gates, ys):
    n = slots.shape[1]
    w = ys.shape[1]
    ch = SC_COMBINE_CHUNK
    lanes = SC_LANES
    per_worker = n // (SC_CORES * SC_SUBCORES)
    n_chunks = per_worker // ch
    assert n_chunks % 2 == 0
    mesh = plsc.VectorSubcoreMesh(core_axis_name="c", subcore_axis_name="s")

    @functools.partial(
        pl.kernel, mesh=mesh,
        out_type=jax.ShapeDtypeStruct((n, w), jnp.int32),
        scratch_types=[pltpu.VMEM((TOP_K, per_worker), jnp.int32),
                       pltpu.VMEM((TOP_K, per_worker), jnp.float32),
                       pltpu.VMEM((2, TOP_K, ch, w), jnp.int32), pltpu.VMEM((2, ch, w), jnp.int32),
                       pltpu.SemaphoreType.DMA((2,)), pltpu.SemaphoreType.DMA((2,))],
        compiler_params=dataclasses.replace(pltpu.CompilerParams(), needs_layout_passes=False),
        name="sc_combine")
    def k(idx_hbm, g_hbm, ys_hbm, out_hbm, idx_v, g_v, rows_v, out_v, gather_sem, out_sem):
        wid = lax.axis_index("s") * SC_CORES + lax.axis_index("c")
        tok0 = wid * per_worker
        for kk in range(TOP_K):
            pltpu.sync_copy(idx_hbm.at[kk, pl.ds(tok0, per_worker)], idx_v.at[kk])
            pltpu.sync_copy(g_hbm.at[kk, pl.ds(tok0, per_worker)], g_v.at[kk])

        def gathers(chunk, b):
            return [pltpu.make_async_copy(ys_hbm.at[idx_v.at[kk, pl.ds(chunk * ch, ch)]],
                                          rows_v.at[b, kk], gather_sem.at[b])
                    for kk in range(TOP_K)]

        def writeback(chunk, b):
            return pltpu.make_async_copy(out_v.at[b], out_hbm.at[pl.ds(tok0 + chunk * ch, ch)],
                                         out_sem.at[b])

        def weighted_sum(chunk, b):
            @pl.loop(0, ch)
            def _(t):
                tok = jnp.full((lanes,), chunk * ch + t, jnp.int32)
                gate = [plsc.load_gather(g_v, [jnp.full((lanes,), kk, jnp.int32), tok])
                        for kk in range(TOP_K)]

                @plsc.parallel_loop(0, w // lanes, 1, unroll=SC_COMBINE_UNROLL)
                def _(j):
                    sl = pl.ds(j * lanes, lanes)
                    acc_lo = jnp.zeros((lanes,), jnp.float32)
                    acc_hi = jnp.zeros((lanes,), jnp.float32)
                    for kk in range(TOP_K):
                        word = rows_v[b, kk, t, sl]
                        lo = plsc.bitcast(word << 16, jnp.float32)
                        hi = plsc.bitcast(word & jnp.int32(-65536), jnp.float32)
                        acc_lo = acc_lo + gate[kk] * lo
                        acc_hi = acc_hi + gate[kk] * hi
                    packed = plsc.pack(acc_lo, acc_hi, format=plsc.PackFormat.INTERLEAVED)
                    out_v[b, t, sl] = plsc.bitcast(packed, jnp.int32)

        for cp in gathers(0, 0):
            cp.start()

        @pl.loop(0, n_chunks, step=2)
        def _(c):
            for b in range(2):
                cur = c + b

                @pl.when(cur + 1 < n_chunks)
                def _():
                    for cp in gathers(cur + 1, 1 - b):
                        cp.start()

                for cp in gathers(cur, b):
                    cp.wait()

                @pl.when(cur >= 2)
                def _():
                    writeback(cur - 2, b).wait()

                weighted_sum(cur, b)
                writeback(cur, b).start()

        writeback(n_chunks - 2, 0).wait()
        writeback(n_chunks - 1, 1).wait()

    return k(slots, gates, ys)


def _rotary_tables(seq_len):
    d = HEAD_DIM
    inv_freq = np.float32(ROPE_BASE) ** (-np.arange(0, d, 2, dtype=np.float32) / np.float32(d))
    ang = np.arange(seq_len, dtype=np.float32)[:, None] * inv_freq[None, :].astype(np.float32)
    cos = np.cos(ang).astype(np.float32)
    sin = np.sin(ang).astype(np.float32)
    return (jnp.asarray(np.concatenate([cos, cos], axis=1)),
            jnp.asarray(np.concatenate([-sin, sin], axis=1)))


def _trunk(x, p):
    bsz, seq_len, _ = x.shape
    n = bsz * seq_len
    x2 = x.reshape(n, D_MODEL)
    cos_t, sin_t = _rotary_tables(seq_len)

    q, k, v, sg, u, st = _ln_proj(x2, p["ln_in_g"], p["ln_in_b"], p["w_in"], cos_t, sin_t, seq_len)
    shp = (bsz, seq_len, RET_WIDTH)
    ret = _retention(q.reshape(shp), k.reshape(shp), v.reshape(shp), sg.reshape(shp),
                     p["dl_f"], p["dl_b"]).reshape(n, RET_WIDTH)

    h2, h2p, idx8, rank8, gates8, cnt = _mix_router(
        x2, st, ret, u, p["ln_in_g"], p["ln_in_b"], p["w_pool"], p["pool_scale"], p["w_o"],
        p["ln_mix_g"], p["ln_mix_b"], p["wr_hi"], p["wr_cat"], p["br_full"], seq_len)

    starts, items = _expert_layout(cnt[:, 0], n * TOP_K)
    idx = idx8[:TOP_K]
    eq = idx[None, :, :] == jnp.arange(N_EXPERTS, dtype=jnp.int32)[:, None, None]
    slots = rank8[:TOP_K] + jnp.sum(jnp.where(eq, starts[:, None, None], 0), axis=0)

    xs = _sc_dispatch(slots, h2p, _padded_tiles(n * TOP_K) * TM_EXPERT)
    ys = _experts(items, xs, p["w_gate"], p["b_gate"], p["w_up"], p["b_up"],
                  p["w_down"], p["b_down"])
    f_packed = _sc_combine(slots, gates8, ys)
    out = _combine(h2, p["ln_ffn_g"], p["ln_ffn_b"], f_packed)
    return out.reshape(bsz, seq_len, D_MODEL)


def kernel(x_prompt, x_sample, ln_in_g, ln_in_b, w_in, decay_logit_fwd, decay_logit_bwd, w_pool,
           pool_scale, w_o, ln_mix_g, ln_mix_b, w_router, b_router, w_gate, b_gate, w_up, b_up,
           w_down, b_down, ln_ffn_g, ln_ffn_b):
    row = lambda a: a.reshape(1, -1)
    wr = w_router[0]
    wr_hi = _bf16(wr)
    wr_lo = _bf16(wr - wr_hi.astype(jnp.float32))
    lane_pad = jnp.zeros((D_MODEL, 128 - 2 * N_EXPERTS), jnp.bfloat16)
    wr_cat = jnp.concatenate([wr_hi, wr_lo, lane_pad], axis=1)
    wr_hi = jnp.concatenate([wr_hi, jnp.zeros_like(wr_hi), lane_pad], axis=1)
    eye = jnp.eye(len(POOL_WINDOWS), dtype=jnp.float32)
    w_pool_bd = (eye[:, None, :, None] * w_pool[0][:, :, None, :]).reshape(POOL_WIDTH, POOL_WIDTH)
    p = {
        "ln_in_g": row(ln_in_g), "ln_in_b": row(ln_in_b),
        "w_in": _bf16(w_in[0]),
        "dl_f": decay_logit_fwd[0], "dl_b": decay_logit_bwd[0],
        "w_pool": _bf16(w_pool_bd), "pool_scale": row(pool_scale[0]),
        "w_o": _bf16(w_o[0]),
        "ln_mix_g": row(ln_mix_g[0]), "ln_mix_b": row(ln_mix_b[0]),
        "wr_hi": wr_hi, "wr_cat": wr_cat,
        "br_full": jnp.broadcast_to(b_router[0][:, None], (N_EXPERTS, MIX_SUB)),
        "w_gate": w_gate[0], "b_gate": b_gate[0][:, None, :],
        "w_up": w_up[0], "b_up": b_up[0][:, None, :],
        "w_down": w_down[0], "b_down": b_down[0][:, None, :],
        "ln_ffn_g": row(ln_ffn_g[0]), "ln_ffn_b": row(ln_ffn_b[0]),
    }
    return (_trunk(x_prompt, p), _trunk(x_sample, p))
```

```python
import dataclasses
import functools

import jax
import jax.numpy as jnp
import numpy as np
from jax import lax
from jax.experimental import pallas as pl
from jax.experimental.pallas import tpu as pltpu
from jax.experimental.pallas import tpu_sc as plsc

D_MODEL = 1024
RET_WIDTH = 512
POOL_WIDTH = 512
N_HEADS = 4
HEAD_DIM = 128
ROPE_BASE = 10000.0
POOL_WINDOWS = (2, 4, 8, 16)
POOL_GROUP = 128
IN_WIDTH = 4 * RET_WIDTH + POOL_WIDTH
N_EXPERTS = 32
TOP_K = 4
SWIGLU_ALPHA = 1.702
SWIGLU_LIMIT = 7.0
LN_EPS = 1e-5
DN_ALPHA = 2.0 ** 0.25

TM_PROJ = 1024
RET_TILE = 2048
RET_CHUNK = 256
TM_MIX = 1024
POOL_HALO = 16
TM_EXPERT = 512
TM_COMBINE = 1024

VMEM_LIMIT = 56 * 1024 * 1024

_NT = (((1,), (1,)), ((), ()))
_TN = (((0,), (0,)), ((), ()))


def _ln_stats(x):
    mu = jnp.mean(x, axis=-1, keepdims=True)
    xc = x - mu
    var = jnp.mean(xc * xc, axis=-1, keepdims=True)
    return mu, lax.rsqrt(var + LN_EPS)


def _layer_norm(x, g, b):
    mu, rstd = _ln_stats(x)
    return (x - mu) * rstd * g + b


def _bf16(x):
    return x.astype(jnp.bfloat16)


def _dot(a, b):
    return jnp.dot(a, b, preferred_element_type=jnp.float32)


HALF = D_MODEL // 2


def _pack_rows(x):
    return _pack_rounded(_bf16(x).astype(jnp.float32))


def _pack_rounded(xr):
    lo = lax.bitcast_convert_type(xr[:, :HALF], jnp.uint32)
    hi = lax.bitcast_convert_type(xr[:, HALF:], jnp.uint32)
    return lax.bitcast_convert_type((lo >> 16) | hi, jnp.int32)


def _unpack_rows(w):
    u = lax.bitcast_convert_type(w, jnp.uint32)
    lo = lax.bitcast_convert_type(u << 16, jnp.float32)
    hi = lax.bitcast_convert_type(u & jnp.uint32(0xFFFF0000), jnp.float32)
    return lo, hi


def _ln_proj_kernel(x_ref, g_ref, b_ref, w_ref, cos_ref, sin_ref,
                    q_ref, k_ref, v_ref, sg_ref, u_ref, st_ref):
    x = x_ref[...]
    mu, rstd = _ln_stats(x)
    lane = lax.broadcasted_iota(jnp.int32, st_ref.shape, 1)
    st_ref[...] = jnp.where(lane == 0, mu, jnp.where(lane == 1, rstd, 0.0))
    h = _bf16((x - mu) * rstd * g_ref[...] + b_ref[...])
    cos = cos_ref[...]
    sin = sin_ref[...]

    def rotary(t):
        return t * cos + pltpu.roll(t, HEAD_DIM // 2, axis=1) * sin

    R = RET_WIDTH
    pq = _dot(h, w_ref[:, 0:R])
    for hd in range(N_HEADS):
        sl = slice(hd * HEAD_DIM, (hd + 1) * HEAD_DIM)
        q_ref[:, sl] = _bf16(rotary(pq[:, sl]))
    pk = _dot(h, w_ref[:, R:2 * R])
    for hd in range(N_HEADS):
        sl = slice(hd * HEAD_DIM, (hd + 1) * HEAD_DIM)
        k_ref[:, sl] = _bf16(rotary(pk[:, sl]) * (HEAD_DIM ** -0.5))
    v_ref[...] = _bf16(_dot(h, w_ref[:, 2 * R:3 * R]))
    pg = _dot(h, w_ref[:, 3 * R:4 * R])
    sg_ref[...] = _bf16(pg * jax.nn.sigmoid(pg))
    u_ref[...] = _bf16(_dot(h, w_ref[:, 4 * R:]))


def _ln_proj(x2, ln_g, ln_b, w_in_bf, cos_t, sin_t, seq_len):
    n = x2.shape[0]
    tm = TM_PROJ
    tiles_per_seq = seq_len // tm
    row = lambda i: (i, 0)
    const = lambda i: (0, 0)
    out_sd = jax.ShapeDtypeStruct((n, RET_WIDTH), jnp.bfloat16)
    return pl.pallas_call(
        _ln_proj_kernel,
        grid=(n // tm,),
        in_specs=[
            pl.BlockSpec((tm, D_MODEL), row),
            pl.BlockSpec((1, D_MODEL), const),
            pl.BlockSpec((1, D_MODEL), const),
            pl.BlockSpec((D_MODEL, IN_WIDTH), const),
            pl.BlockSpec((tm, HEAD_DIM), lambda i: (i % tiles_per_seq, 0)),
            pl.BlockSpec((tm, HEAD_DIM), lambda i: (i % tiles_per_seq, 0)),
        ],
        out_specs=[pl.BlockSpec((tm, RET_WIDTH), row)] * 5 + [pl.BlockSpec((tm, 128), row)],
        out_shape=[out_sd] * 5 + [jax.ShapeDtypeStruct((n, 128), jnp.float32)],
        compiler_params=pltpu.CompilerParams(
            dimension_semantics=("arbitrary",), vmem_limit_bytes=VMEM_LIMIT),
        name="ln_proj",
    )(x2, ln_g, ln_b, w_in_bf, cos_t, sin_t)


def _log_sigmoid(x):
    return jnp.minimum(x, 0.0) - jnp.log(1.0 + jnp.exp(-jnp.abs(x)))


def _retention_kernel(dlf_ref, dlb_ref, q_ref, k_ref, v_ref, sg_ref, o_ref,
                      dmat, xi_f, xi_b, zeta_f, zeta_b, cdec, s_f, s_b, snap):
    b = pl.program_id(0)
    phase = pl.program_id(1)
    j = pl.program_id(2)
    n_tiles = pl.num_programs(2)
    C = RET_CHUNK
    sub = RET_TILE // C

    @pl.when((b == 0) & (phase == 0) & (j == 0))
    def _init_tables():
        ri = lax.broadcasted_iota(jnp.int32, (C, C), 0).astype(jnp.float32)
        ci = lax.broadcasted_iota(jnp.int32, (C, C), 1).astype(jnp.float32)
        rel = ri - ci
        pos = lax.broadcasted_iota(jnp.int32, (C, HEAD_DIM), 0).astype(jnp.float32)
        zero_cc = jnp.zeros((C, C), jnp.float32)
        zero_cd = jnp.zeros((C, HEAD_DIM), jnp.float32)
        zero_dd = jnp.zeros((HEAD_DIM, HEAD_DIM), jnp.float32)
        for hd in range(N_HEADS):
            dlf = dlf_ref[hd]
            dlb = dlb_ref[hd]
            dmat[hd] = jnp.where(rel >= 0, jnp.exp(_log_sigmoid(zero_cc + dlf) * rel),
                                 jnp.exp(-_log_sigmoid(zero_cc + dlb) * rel))
            lf = _log_sigmoid(zero_cd + dlf)
            lb = _log_sigmoid(zero_cd + dlb)
            xi_f[hd] = jnp.exp(lf * (pos + 1.0))
            xi_b[hd] = jnp.exp(lb * (C - pos))
            zeta_f[hd] = jnp.exp(lf * (C - 1.0 - pos))
            zeta_b[hd] = jnp.exp(lb * pos)
            cdec[hd, 0] = jnp.exp(_log_sigmoid(zero_dd + dlf) * float(C))
            cdec[hd, 1] = jnp.exp(_log_sigmoid(zero_dd + dlb) * float(C))

    @pl.when(j == 0)
    def _reset_state():
        @pl.when(phase == 0)
        def _():
            s_b[...] = jnp.zeros_like(s_b)

        @pl.when(phase == 1)
        def _():
            s_f[...] = jnp.zeros_like(s_f)

    @pl.when(phase == 0)
    def _backward_states():
        tile = n_tiles - 1 - j

        def body(cc, carry):
            c = sub - 1 - cc
            r0 = pl.multiple_of(c * C, C)
            chunk = tile * sub + c
            for hd in range(N_HEADS):
                sl = slice(hd * HEAD_DIM, (hd + 1) * HEAD_DIM)
                kc = k_ref[0, pl.ds(r0, C), sl]
                vc = v_ref[0, pl.ds(r0, C), sl].astype(jnp.float32)
                st = s_b[hd]
                snap[chunk * N_HEADS + hd] = _bf16(st)
                kv = lax.dot_general(kc, _bf16(vc * zeta_b[hd]), _TN,
                                     preferred_element_type=jnp.float32)
                s_b[hd] = st * cdec[hd, 1] + kv
            return carry

        lax.fori_loop(0, sub, body, 0, unroll=True)

    @pl.when(phase == 1)
    def _forward_outputs():
        def body(c, carry):
            r0 = pl.multiple_of(c * C, C)
            chunk = j * sub + c
            for hd in range(N_HEADS):
                sl = slice(hd * HEAD_DIM, (hd + 1) * HEAD_DIM)
                qc = q_ref[0, pl.ds(r0, C), sl]
                kc = k_ref[0, pl.ds(r0, C), sl]
                vc = v_ref[0, pl.ds(r0, C), sl]
                scores = lax.dot_general(qc, kc, _NT, preferred_element_type=jnp.float32)
                o = _dot(_bf16(scores * dmat[hd]), vc)
                st = s_f[hd]
                both = jnp.concatenate([_bf16(st), snap[chunk * N_HEADS + hd]], axis=1)
                cross = _dot(qc, both)
                o = o + xi_f[hd] * cross[:, :HEAD_DIM] + xi_b[hd] * cross[:, HEAD_DIM:]
                kv = lax.dot_general(kc, _bf16(vc.astype(jnp.float32) * zeta_f[hd]), _TN,
                                     preferred_element_type=jnp.float32)
                s_f[hd] = st * cdec[hd, 0] + kv
                mu = jnp.mean(o, axis=-1, keepdims=True)
                oc = o - mu
                on = oc * lax.rsqrt(jnp.mean(oc * oc, axis=-1, keepdims=True) + LN_EPS)
                sg = sg_ref[0, pl.ds(r0, C), sl].astype(jnp.float32)
                o_ref[0, pl.ds(r0, C), sl] = _bf16(sg * on)
            return carry

        lax.fori_loop(0, sub, body, 0, unroll=True)


def _retention(q, k, v, sg, dl_f, dl_b):
    bsz, seq_len, _ = q.shape
    n_tiles = seq_len // RET_TILE
    n_chunks = seq_len // RET_CHUNK
    C = RET_CHUNK

    def kv_map(b, p, j):
        return (b, jnp.where(p == 0, n_tiles - 1 - j, j), 0)

    def q_map(b, p, j):
        return (b, jnp.where(p == 0, 0, j), 0)

    blk = (1, RET_TILE, RET_WIDTH)
    smem = pl.BlockSpec(memory_space=pltpu.SMEM)
    return pl.pallas_call(
        _retention_kernel,
        grid=(bsz, 2, n_tiles),
        in_specs=[smem, smem,
                  pl.BlockSpec(blk, q_map), pl.BlockSpec(blk, kv_map),
                  pl.BlockSpec(blk, kv_map), pl.BlockSpec(blk, q_map)],
        out_specs=pl.BlockSpec(blk, q_map),
        out_shape=jax.ShapeDtypeStruct(q.shape, jnp.bfloat16),
        scratch_shapes=[
            pltpu.VMEM((N_HEADS, C, C), jnp.float32),
            pltpu.VMEM((N_HEADS, C, HEAD_DIM), jnp.float32),
            pltpu.VMEM((N_HEADS, C, HEAD_DIM), jnp.float32),
            pltpu.VMEM((N_HEADS, C, HEAD_DIM), jnp.float32),
            pltpu.VMEM((N_HEADS, C, HEAD_DIM), jnp.float32),
            pltpu.VMEM((N_HEADS, 2, HEAD_DIM, HEAD_DIM), jnp.float32),
            pltpu.VMEM((N_HEADS, HEAD_DIM, HEAD_DIM), jnp.float32),
            pltpu.VMEM((N_HEADS, HEAD_DIM, HEAD_DIM), jnp.float32),
            pltpu.VMEM((n_chunks * N_HEADS, HEAD_DIM, HEAD_DIM), jnp.bfloat16),
        ],
        compiler_params=pltpu.CompilerParams(
            dimension_semantics=("arbitrary", "arbitrary", "arbitrary"),
            vmem_limit_bytes=VMEM_LIMIT),
        name="retention",
    )(dl_f, dl_b, q, k, v, sg)


def _mix_router_kernel(seq_len, x_ref, st_ref, ret_ref, u_ref, up_ref, un_ref, inv_ref,
                       lng_ref, lnb_ref, wpool_ref, pscale_ref, wo_ref,
                       mg_ref, mb_ref, wr_hi_ref, wr_cat_ref, br_ref,
                       h2_ref, h2p_ref, idx_ref, rank_ref, gate_ref, cnt_ref,
                       tri, base):
    i = pl.program_id(0)
    tm = TM_MIX
    halo = POOL_HALO
    tiles_per_seq = seq_len // tm
    ts = i % tiles_per_seq

    @pl.when(i == 0)
    def _init():
        tr = lax.broadcasted_iota(jnp.int32, (tm, tm), 0)
        tc = lax.broadcasted_iota(jnp.int32, (tm, tm), 1)
        tri[...] = jnp.where(tr < tc, 1.0, 0.0).astype(jnp.bfloat16)
        base[...] = jnp.zeros_like(base)

    prev = jnp.where(ts == 0, jnp.zeros_like(up_ref[...]), up_ref[...])
    nxt = jnp.where(ts == tiles_per_seq - 1, jnp.zeros_like(un_ref[...]), un_ref[...])
    u_cur = u_ref[...]
    u_ext = jnp.concatenate([prev, u_cur, nxt], axis=0).astype(jnp.float32)
    n_ext = tm + 2 * halo

    sums = []
    for gi, w in enumerate(POOL_WINDOWS):
        a = u_ext[:, gi * POOL_GROUP:(gi + 1) * POOL_GROUP]
        span = 1
        while span < w // 2:
            a = a + pltpu.roll(a, n_ext - span, axis=0)
            span *= 2
        sums.append(a + pltpu.roll(a, w // 2, axis=0))
    wsum = jnp.concatenate(sums, axis=1)[halo:halo + tm, :]
    pooled = wsum * inv_ref[0] - u_cur.astype(jnp.float32)
    pool = _bf16(_dot(_bf16(pooled), wpool_ref[...]) * pscale_ref[...])

    m = _dot(ret_ref[...], wo_ref[0:RET_WIDTH, :]) + _dot(pool, wo_ref[RET_WIDTH:, :])
    st = st_ref[...]
    h = (x_ref[...] - st[:, 0:1]) * st[:, 1:2] * lng_ref[...] + lnb_ref[...]
    h2 = _layer_norm(DN_ALPHA * h + m, mg_ref[...], mb_ref[...])
    h2_ref[...] = h2
    h_hi = _bf16(h2)
    h_hi32 = h_hi.astype(jnp.float32)
    h2p_ref[...] = _pack_rounded(h_hi32)

    h_lo = _bf16(h2 - h_hi32)
    part = _dot(h_hi, wr_cat_ref[...])
    lrow = part + pltpu.roll(part, 128 - N_EXPERTS, axis=1) + _dot(h_lo, wr_hi_ref[...])
    logits = lrow.T[0:N_EXPERTS, :] + br_ref[...]

    e_iota = lax.broadcasted_iota(jnp.int32, (N_EXPERTS, tm), 0)
    k_iota = lax.broadcasted_iota(jnp.int32, (8, tm), 0)
    vals8 = jnp.zeros((8, tm), jnp.float32)
    idx8 = jnp.zeros((8, tm), jnp.int32)
    sels = []
    work = logits
    for kk in range(TOP_K):
        mval = jnp.max(work, axis=0, keepdims=True)
        midx = jnp.min(jnp.where(work == mval, e_iota, N_EXPERTS), axis=0, keepdims=True)
        sel = e_iota == midx
        sels.append(sel)
        vals8 = jnp.where(k_iota == kk, mval, vals8)
        idx8 = jnp.where(k_iota == kk, midx, idx8)
        work = jnp.where(sel, -jnp.inf, work)

    ex = jnp.where(k_iota < TOP_K, jnp.exp(vals8 - vals8[0:1, :]), 0.0)
    gate_ref[...] = ex / jnp.sum(ex, axis=0, keepdims=True)

    onehot = jnp.zeros((N_EXPERTS, tm), jnp.float32)
    for sel in sels:
        onehot = onehot + jnp.where(sel, 1.0, 0.0)
    cum = _dot(_bf16(onehot), tri[...])
    before = base[...] + cum
    rank8 = jnp.zeros((8, tm), jnp.int32)
    for kk, sel in enumerate(sels):
        rk = jnp.sum(jnp.where(sel, before, 0.0), axis=0, keepdims=True).astype(jnp.int32)
        rank8 = jnp.where(k_iota == kk, rk, rank8)
    total = cum[:, tm - 1:tm] + onehot[:, tm - 1:tm]
    running = base[...] + jnp.broadcast_to(total, (N_EXPERTS, tm))
    base[...] = running

    idx_ref[...] = idx8
    rank_ref[...] = rank8
    cnt_ref[...] = running[:, 0:128].astype(jnp.int32)


def _pool_inv_counts(seq_len):
    tm = TM_MIX
    assert seq_len // tm >= 3
    cols = []
    for w in POOL_WINDOWS:
        pos = np.arange(seq_len)
        lo = np.clip(pos - w // 2, 0, seq_len - 1)
        hi = np.clip(pos + (w - 1 - w // 2), 0, seq_len - 1)
        inv = (1.0 / (hi - lo + 1)).astype(np.float32)
        kinds = np.stack([inv[:tm], inv[tm:2 * tm], inv[seq_len - tm:]])
        cols.append(np.repeat(kinds[:, :, None], POOL_GROUP, axis=2))
    return jnp.asarray(np.concatenate(cols, axis=2))


def _mix_router(x2, st, ret, u, ln_g, ln_b, w_pool_bf, pool_scale, w_o_bf, mix_g, mix_b,
                wr_hi, wr_cat, br_full, seq_len):
    n = x2.shape[0]
    tm = TM_MIX
    halo = POOL_HALO
    hb = tm // halo
    n_halo_blocks = n // halo
    tiles_per_seq = seq_len // tm
    row = lambda i: (i, 0)
    const = lambda i: (0, 0)
    col = lambda i: (0, i)

    def inv_map(i):
        ts = i % tiles_per_seq
        return (jnp.where(ts == 0, 0, jnp.where(ts == tiles_per_seq - 1, 2, 1)), 0, 0)

    return pl.pallas_call(
        functools.partial(_mix_router_kernel, seq_len),
        grid=(n // tm,),
        in_specs=[
            pl.BlockSpec((tm, D_MODEL), row),
            pl.BlockSpec((tm, 128), row),
            pl.BlockSpec((tm, RET_WIDTH), row),
            pl.BlockSpec((tm, POOL_WIDTH), row),
            pl.BlockSpec((halo, POOL_WIDTH), lambda i: (jnp.maximum(i * hb - 1, 0), 0)),
            pl.BlockSpec((halo, POOL_WIDTH),
                         lambda i: (jnp.minimum((i + 1) * hb, n_halo_blocks - 1), 0)),
            pl.BlockSpec((1, tm, POOL_WIDTH), inv_map),
            pl.BlockSpec((1, D_MODEL), const),
            pl.BlockSpec((1, D_MODEL), const),
            pl.BlockSpec((POOL_WIDTH, POOL_WIDTH), const),
            pl.BlockSpec((1, POOL_WIDTH), const),
            pl.BlockSpec((D_MODEL, D_MODEL), const),
            pl.BlockSpec((1, D_MODEL), const),
            pl.BlockSpec((1, D_MODEL), const),
            pl.BlockSpec((D_MODEL, 128), const),
            pl.BlockSpec((D_MODEL, 128), const),
            pl.BlockSpec((N_EXPERTS, tm), const),
        ],
        out_specs=[
            pl.BlockSpec((tm, D_MODEL), row),
            pl.BlockSpec((tm, HALF), row),
            pl.BlockSpec((8, tm), col),
            pl.BlockSpec((8, tm), col),
            pl.BlockSpec((8, tm), col),
            pl.BlockSpec((N_EXPERTS, 128), const),
        ],
        out_shape=[
            jax.ShapeDtypeStruct((n, D_MODEL), jnp.float32),
            jax.ShapeDtypeStruct((n, HALF), jnp.int32),
            jax.ShapeDtypeStruct((8, n), jnp.int32),
            jax.ShapeDtypeStruct((8, n), jnp.int32),
            jax.ShapeDtypeStruct((8, n), jnp.float32),
            jax.ShapeDtypeStruct((N_EXPERTS, 128), jnp.int32),
        ],
        scratch_shapes=[
            pltpu.VMEM((tm, tm), jnp.bfloat16),
            pltpu.VMEM((N_EXPERTS, tm), jnp.float32),
        ],
        compiler_params=pltpu.CompilerParams(
            dimension_semantics=("arbitrary",), vmem_limit_bytes=VMEM_LIMIT),
        name="mix_router",
    )(x2, st, ret, u, u, u, _pool_inv_counts(seq_len), ln_g, ln_b, w_pool_bf, pool_scale, w_o_bf,
      mix_g, mix_b, wr_hi, wr_cat, br_full)


def _expert_kernel(it_tile, it_exp, it_rows, it_new, it_slot, it_next,
                   xs_ref, wg_hbm, bg_ref, wu_hbm, bu_ref, wd_hbm, bd_ref, ys_ref,
                   stage, wg_bf, wu_bf, wd_bf, sem):
    i = pl.program_id(0)
    n_valid = it_rows[i]
    w_hbm = (wg_hbm, wu_hbm, wd_hbm)
    w_bf = (wg_bf, wu_bf, wd_bf)

    def weight_copies(expert, slot):
        return [pltpu.make_async_copy(w_hbm[m].at[expert], stage.at[slot, m], sem.at[slot, m])
                for m in range(3)]

    @pl.when(i == 0)
    def _first_fetch():
        for c in weight_copies(it_exp[0], 0):
            c.start()

    @pl.when(it_new[i] == 1)
    def _switch_expert():
        slot = it_slot[i]
        for c in weight_copies(it_exp[i], slot):
            c.wait()
        nxt = it_next[i]

        @pl.when(nxt >= 0)
        def _():
            for c in weight_copies(nxt, 1 - slot):
                c.start()

        for m in range(3):
            w_bf[m][...] = _bf16(stage[slot, m])

    @pl.when(n_valid > 0)
    def _ffn():
        r = lax.broadcasted_iota(jnp.int32, (TM_EXPERT, HALF), 0)
        x_lo, x_hi = _unpack_rows(jnp.where(r < n_valid, xs_ref[...], 0))
        x = jnp.concatenate([_bf16(x_lo), _bf16(x_hi)], axis=1)
        gt = _dot(x, wg_bf[...]) + bg_ref[0]
        up = _dot(x, wu_bf[...]) + bu_ref[0]
        gt = jnp.minimum(gt, SWIGLU_LIMIT)
        up = jnp.clip(up, -SWIGLU_LIMIT, SWIGLU_LIMIT)
        hid = (up + 1.0) * (gt * jax.nn.sigmoid(SWIGLU_ALPHA * gt))
        ys_ref[...] = _pack_rows(_dot(_bf16(hid), wd_bf[...]) + bd_ref[0])


def _experts(items, xs, w_gate, b_gate, w_up, b_up, w_down, b_down):
    it_tile, it_exp, it_rows, it_new, it_slot, it_next = items
    p = xs.shape[0]
    tm = TM_EXPERT
    n_items = it_tile.shape[0]
    tile_map = lambda i, t, e, *_: (t[i], 0)
    b_spec = pl.BlockSpec((1, 1, D_MODEL), lambda i, t, e, *_: (e[i], 0, 0))
    w_spec = pl.BlockSpec(memory_space=pl.ANY)
    w_scratch = pltpu.VMEM((D_MODEL, D_MODEL), jnp.bfloat16)
    grid_spec = pltpu.PrefetchScalarGridSpec(
        num_scalar_prefetch=6,
        grid=(n_items,),
        in_specs=[pl.BlockSpec((tm, HALF), tile_map),
                  w_spec, b_spec, w_spec, b_spec, w_spec, b_spec],
        out_specs=pl.BlockSpec((tm, HALF), tile_map),
        scratch_shapes=[pltpu.VMEM((2, 3, D_MODEL, D_MODEL), jnp.float32),
                        w_scratch, w_scratch, w_scratch,
                        pltpu.SemaphoreType.DMA((2, 3))],
    )
    return pl.pallas_call(
        _expert_kernel,
        grid_spec=grid_spec,
        out_shape=jax.ShapeDtypeStruct((p, HALF), jnp.int32),
        compiler_params=pltpu.CompilerParams(
            dimension_semantics=("arbitrary",), vmem_limit_bytes=VMEM_LIMIT),
        name="experts",
    )(it_tile, it_exp, it_rows, it_new, it_slot, it_next,
      xs, w_gate, b_gate, w_up, b_up, w_down, b_down)


def _padded_tiles(n_rows):
    return n_rows // TM_EXPERT + N_EXPERTS


def _expert_layout(counts, n_rows):
    tm = TM_EXPERT
    n_items = _padded_tiles(n_rows)
    tiles_e = (counts + tm - 1) // tm
    tile_end = jnp.cumsum(tiles_e)
    tile_start = tile_end - tiles_e
    total = tile_end[-1]
    i = jnp.arange(n_items, dtype=jnp.int32)
    tile = jnp.minimum(i, total - 1)
    e = jnp.sum((tile[:, None] >= tile_end[None, :]).astype(jnp.int32), axis=1)
    e = jnp.minimum(e, N_EXPERTS - 1)
    onehot = (e[:, None] == jnp.arange(N_EXPERTS, dtype=jnp.int32)[None, :]).astype(jnp.int32)
    pick = lambda a: jnp.sum(onehot * a[None, :], axis=1)
    first = pick(tile_start)
    valid = i < total
    rows = jnp.where(valid, jnp.clip(pick(counts) - (tile - first) * tm, 0, tm), 0)
    new = valid & (tile == first)
    slot = jnp.maximum(jnp.cumsum(new.astype(jnp.int32)) - 1, 0) % 2
    ids = jnp.arange(N_EXPERTS, dtype=jnp.int32)
    later = jnp.where((counts[None, :] > 0) & (ids[None, :] > ids[:, None]), ids[None, :], N_EXPERTS)
    next_e = jnp.min(later, axis=1)
    nxt = pick(jnp.where(next_e < N_EXPERTS, next_e, -1))
    items = tuple(a.astype(jnp.int32) for a in (tile, e, rows, new, slot, nxt))
    return tile_start * tm, items


def _combine_kernel(h2_ref, g_ref, b_ref, f_ref, o_ref):
    f_lo, f_hi = _unpack_rows(f_ref[...])
    y = DN_ALPHA * h2_ref[...] + jnp.concatenate([f_lo, f_hi], axis=1)
    o_ref[...] = _layer_norm(y, g_ref[...], b_ref[...])


def _combine(h2, ffn_g, ffn_b, f_packed):
    n = h2.shape[0]
    tm = TM_COMBINE
    row = lambda i: (i, 0)
    const = lambda i: (0, 0)
    return pl.pallas_call(
        _combine_kernel,
        grid=(n // tm,),
        in_specs=[
            pl.BlockSpec((tm, D_MODEL), row),
            pl.BlockSpec((1, D_MODEL), const),
            pl.BlockSpec((1, D_MODEL), const),
            pl.BlockSpec((tm, HALF), row),
        ],
        out_specs=pl.BlockSpec((tm, D_MODEL), row),
        out_shape=jax.ShapeDtypeStruct((n, D_MODEL), jnp.float32),
        compiler_params=pltpu.CompilerParams(
            dimension_semantics=("arbitrary",), vmem_limit_bytes=VMEM_LIMIT),
        name="combine",
    )(h2, ffn_g, ffn_b, f_packed)


SC_CORES = 2
SC_SUBCORES = 16
SC_LANES = 16
SC_CHUNK = 64
SC_COMBINE_CHUNK = 16
SC_COMBINE_UNROLL = 4


def _sc_dispatch(slots, h2, n_out_rows):
    n, w = h2.shape
    ch = SC_CHUNK
    n_chunks = n // (SC_CORES * SC_SUBCORES * ch)
    mesh = plsc.VectorSubcoreMesh(core_axis_name="c", subcore_axis_name="s")

    @functools.partial(
        pl.kernel, mesh=mesh,
        out_type=jax.ShapeDtypeStruct((n_out_rows, w), h2.dtype),
        scratch_types=[pltpu.VMEM((TOP_K, ch), jnp.int32), pltpu.VMEM((ch, w), h2.dtype)],
        name="sc_dispatch")
    def k(idx_hbm, h2_hbm, xs_hbm, idx_v, rows_v):
        wid = lax.axis_index("s") * SC_CORES + lax.axis_index("c")

        @pl.loop(0, n_chunks)
        def _(c):
            chunk = wid * n_chunks + c
            for kk in range(TOP_K):
                pltpu.sync_copy(idx_hbm.at[kk, pl.ds(chunk * ch, ch)], idx_v.at[kk])
            pltpu.sync_copy(h2_hbm.at[pl.ds(chunk * ch, ch)], rows_v)
            for kk in range(TOP_K):
                pltpu.sync_copy(rows_v, xs_hbm.at[idx_v.at[kk]])

    return k(slots, h2)


def _sc_combine(slots, gates, ys):
    n = slots.shape[1]
    w = ys.shape[1]
    ch = SC_COMBINE_CHUNK
    lanes = SC_LANES
    per_worker = n // (SC_CORES * SC_SUBCORES)
    n_chunks = per_worker // ch
    assert n_chunks % 2 == 0
    mesh = plsc.VectorSubcoreMesh(core_axis_name="c", subcore_axis_name="s")

    @functools.partial(
        pl.kernel, mesh=mesh,
        out_type=jax.ShapeDtypeStruct((n, w), jnp.int32),
        scratch_types=[pltpu.VMEM((TOP_K, per_worker), jnp.int32),
                       pltpu.VMEM((TOP_K, per_worker), jnp.float32),
                       pltpu.VMEM((2, TOP_K, ch, w), jnp.int32), pltpu.VMEM((2, ch, w), jnp.int32),
                       pltpu.SemaphoreType.DMA((2,)), pltpu.SemaphoreType.DMA((2,))],
        compiler_params=dataclasses.replace(pltpu.CompilerParams(), needs_layout_passes=False),
        name="sc_combine")
    def k(idx_hbm, g_hbm, ys_hbm, out_hbm, idx_v, g_v, rows_v, out_v, gather_sem, out_sem):
        wid = lax.axis_index("s") * SC_CORES + lax.axis_index("c")
        tok0 = wid * per_worker
        for kk in range(TOP_K):
            pltpu.sync_copy(idx_hbm.at[kk, pl.ds(tok0, per_worker)], idx_v.at[kk])
            pltpu.sync_copy(g_hbm.at[kk, pl.ds(tok0, per_worker)], g_v.at[kk])

        def gathers(chunk, b):
            return [pltpu.make_async_copy(ys_hbm.at[idx_v.at[kk, pl.ds(chunk * ch, ch)]],
                                          rows_v.at[b, kk], gather_sem.at[b])
                    for kk in range(TOP_K)]

        def writeback(chunk, b):
            return pltpu.make_async_copy(out_v.at[b], out_hbm.at[pl.ds(tok0 + chunk * ch, ch)],
                                         out_sem.at[b])

        def weighted_sum(chunk, b):
            @pl.loop(0, ch)
            def _(t):
                tok = jnp.full((lanes,), chunk * ch + t, jnp.int32)
                gate = [plsc.load_gather(g_v, [jnp.full((lanes,), kk, jnp.int32), tok])
                        for kk in range(TOP_K)]

                @plsc.parallel_loop(0, w // lanes, 1, unroll=SC_COMBINE_UNROLL)
                def _(j):
                    sl = pl.ds(j * lanes, lanes)
                    acc_lo = jnp.zeros((lanes,), jnp.float32)
                    acc_hi = jnp.zeros((lanes,), jnp.float32)
                    for kk in range(TOP_K):
                        word = rows_v[b, kk, t, sl]
                        lo = plsc.bitcast(word << 16, jnp.float32)
                        hi = plsc.bitcast(word & jnp.int32(-65536), jnp.float32)
                        acc_lo = acc_lo + gate[kk] * lo
                        acc_hi = acc_hi + gate[kk] * hi
                    packed = plsc.pack(acc_lo, acc_hi, format=plsc.PackFormat.INTERLEAVED)
                    out_v[b, t, sl] = plsc.bitcast(packed, jnp.int32)

        for cp in gathers(0, 0):
            cp.start()

        @pl.loop(0, n_chunks, step=2)
        def _(c):
            for b in range(2):
                cur = c + b

                @pl.when(cur + 1 < n_chunks)
                def _():
                    for cp in gathers(cur + 1, 1 - b):
                        cp.start()

                for cp in gathers(cur, b):
                    cp.wait()

                @pl.when(cur >= 2)
                def _():
                    writeback(cur - 2, b).wait()

                weighted_sum(cur, b)
                writeback(cur, b).start()

        writeback(n_chunks - 2, 0).wait()
        writeback(n_chunks - 1, 1).wait()

    return k(slots, gates, ys)


def _rotary_tables(seq_len):
    d = HEAD_DIM
    inv_freq = np.float32(ROPE_BASE) ** (-np.arange(0, d, 2, dtype=np.float32) / np.float32(d))
    ang = np.arange(seq_len, dtype=np.float32)[:, None] * inv_freq[None, :].astype(np.float32)
    cos = np.cos(ang).astype(np.float32)
    sin = np.sin(ang).astype(np.float32)
    return (jnp.asarray(np.concatenate([cos, cos], axis=1)),
            jnp.asarray(np.concatenate([-sin, sin], axis=1)))


def _trunk(x, p):
    bsz, seq_len, _ = x.shape
    n = bsz * seq_len
    x2 = x.reshape(n, D_MODEL)
    cos_t, sin_t = _rotary_tables(seq_len)

    q, k, v, sg, u, st = _ln_proj(x2, p["ln_in_g"], p["ln_in_b"], p["w_in"], cos_t, sin_t, seq_len)
    shp = (bsz, seq_len, RET_WIDTH)
    ret = _retention(q.reshape(shp), k.reshape(shp), v.reshape(shp), sg.reshape(shp),
                     p["dl_f"], p["dl_b"]).reshape(n, RET_WIDTH)

    h2, h2p, idx8, rank8, gates8, cnt = _mix_router(
        x2, st, ret, u, p["ln_in_g"], p["ln_in_b"], p["w_pool"], p["pool_scale"], p["w_o"],
        p["ln_mix_g"], p["ln_mix_b"], p["wr_hi"], p["wr_cat"], p["br_full"], seq_len)

    starts, items = _expert_layout(cnt[:, 0], n * TOP_K)
    idx = idx8[:TOP_K]
    eq = idx[None, :, :] == jnp.arange(N_EXPERTS, dtype=jnp.int32)[:, None, None]
    slots = rank8[:TOP_K] + jnp.sum(jnp.where(eq, starts[:, None, None], 0), axis=0)

    xs = _sc_dispatch(slots, h2p, _padded_tiles(n * TOP_K) * TM_EXPERT)
    ys = _experts(items, xs, p["w_gate"], p["b_gate"], p["w_up"], p["b_up"],
                  p["w_down"], p["b_down"])
    f_packed = _sc_combine(slots, gates8, ys)
    out = _combine(h2, p["ln_ffn_g"], p["ln_ffn_b"], f_packed)
    return out.reshape(bsz, seq_len, D_MODEL)


def kernel(x_prompt, x_sample, ln_in_g, ln_in_b, w_in, decay_logit_fwd, decay_logit_bwd, w_pool,
           pool_scale, w_o, ln_mix_g, ln_mix_b, w_router, b_router, w_gate, b_gate, w_up, b_up,
           w_down, b_down, ln_ffn_g, ln_ffn_b):
    row = lambda a: a.reshape(1, -1)
    wr = w_router[0]
    wr_hi = _bf16(wr)
    wr_lo = _bf16(wr - wr_hi.astype(jnp.float32))
    lane_pad = jnp.zeros((D_MODEL, 128 - 2 * N_EXPERTS), jnp.bfloat16)
    wr_cat = jnp.concatenate([wr_hi, wr_lo, lane_pad], axis=1)
    wr_hi = jnp.concatenate([wr_hi, jnp.zeros_like(wr_hi), lane_pad], axis=1)
    eye = jnp.eye(len(POOL_WINDOWS), dtype=jnp.float32)
    w_pool_bd = (eye[:, None, :, None] * w_pool[0][:, :, None, :]).reshape(POOL_WIDTH, POOL_WIDTH)
    p = {
        "ln_in_g": row(ln_in_g), "ln_in_b": row(ln_in_b),
        "w_in": _bf16(w_in[0]),
        "dl_f": decay_logit_fwd[0], "dl_b": decay_logit_bwd[0],
        "w_pool": _bf16(w_pool_bd), "pool_scale": row(pool_scale[0]),
        "w_o": _bf16(w_o[0]),
        "ln_mix_g": row(ln_mix_g[0]), "ln_mix_b": row(ln_mix_b[0]),
        "wr_hi": wr_hi, "wr_cat": wr_cat,
        "br_full": jnp.broadcast_to(b_router[0][:, None], (N_EXPERTS, TM_MIX)),
        "w_gate": w_gate[0], "b_gate": b_gate[0][:, None, :],
        "w_up": w_up[0], "b_up": b_up[0][:, None, :],
        "w_down": w_down[0], "b_down": b_down[0][:, None, :],
        "ln_ffn_g": row(ln_ffn_g[0]), "ln_ffn_b": row(ln_ffn_b[0]),
    }
    return (_trunk(x_prompt, p), _trunk(x_sample, p))
```
